```python
import jax
import jax.numpy as jnp
from jax import lax
import numpy as np

D_MODEL = 1024
BATCH = 4
SEQ = 8192
DEPTH = 2

ATT_HEADS = 8
ATT_HEAD_DIM = 64
ATT_WIDTH = ATT_HEADS * ATT_HEAD_DIM
IDX_HEADS = 4
IDX_DIM = 64
TOPK_MAX = 256
Q_BLOCK = 128
M_HEADS = 4
M_HEAD_DIM = 128
M_WIDTH = M_HEADS * M_HEAD_DIM
CHUNK = 128
CONV_K = 4
D_FF = 2816
ROPE_THETA = 10000.0
EPS = 1e-6
IN_SIZES = (ATT_WIDTH, ATT_HEAD_DIM, ATT_HEAD_DIM, IDX_HEADS * IDX_DIM, IDX_DIM, IDX_HEADS, 2 * M_WIDTH, M_WIDTH, M_WIDTH, M_HEADS, M_HEADS)
IN_COLS = ATT_WIDTH + 2 * ATT_HEAD_DIM + IDX_HEADS * IDX_DIM + IDX_DIM + IDX_HEADS + 4 * M_WIDTH + 2 * M_HEADS

kernel_name = "hybrid_dsa_mlstm_macaron"


def rmsnorm(x, g):
    xf = x.astype(jnp.float32)
    y = xf * lax.rsqrt(jnp.mean(xf * xf, axis=-1, keepdims=True) + EPS) * g.astype(jnp.float32)
    return y.astype(x.dtype)


def swiglu(x, w_gate, w_up, w_down):
    return (jax.nn.silu(x @ w_gate) * (x @ w_up)) @ w_down


def rope(x, pos):
    half = x.shape[-1] // 2
    inv_freq = ROPE_THETA ** (-jnp.arange(half, dtype=jnp.float32) / half)
    ang = pos.astype(jnp.float32)[:, None] * inv_freq[None, :]
    cos = jnp.cos(ang)[:, None, :]
    sin = jnp.sin(ang)[:, None, :]
    xf = x.astype(jnp.float32)
    x1, x2 = xf[..., :half], xf[..., half:]
    return jnp.concatenate([x1 * cos - x2 * sin, x2 * cos + x1 * sin], axis=-1).astype(x.dtype)


def causal_conv(x, w, b):
    y = lax.conv_general_dilated(x, w.astype(x.dtype), window_strides=(1,), padding=[(CONV_K - 1, 0)],
                                 dimension_numbers=("NWC", "WIO", "NWC"), feature_group_count=x.shape[-1])
    return y + b


def split_columns(proj):
    offsets = np.cumsum(IN_SIZES)[:-1].tolist()
    return jnp.split(proj, offsets, axis=-1)


def dsa_attention(q, k, v, q_idx, k_idx, w_idx):
    B, S = q.shape[0], q.shape[1]
    topk = min(TOPK_MAX, S // 4)
    nb = S // Q_BLOCK
    key_pos = jnp.arange(S, dtype=jnp.int32)

    def to_blocks(a):
        return jnp.moveaxis(a.reshape((B, nb, Q_BLOCK) + a.shape[2:]), 1, 0)

    def gather(src, idx):
        return jax.vmap(lambda sb, ib: sb[ib])(src, idx)

    def one_block(args):
        qb, qib, wb, start = args
        q_pos = start + jnp.arange(Q_BLOCK, dtype=jnp.int32)
        idx_logits = jnp.einsum("bqhd,bsd->bqhs", qib, k_idx, preferred_element_type=jnp.float32) * IDX_DIM ** -0.5
        index_score = jnp.einsum("bqhs,bqh->bqs", jax.nn.relu(idx_logits), wb.astype(jnp.float32))
        index_score = jnp.where(key_pos[None, None, :] <= q_pos[None, :, None], index_score, -jnp.inf)
        _, sel = lax.top_k(index_score, topk)
        k_sel = gather(k, sel)
        v_sel = gather(v, sel)
        logits = jnp.einsum("bqhd,bqkd->bqhk", qb, k_sel, preferred_element_type=jnp.float32) * ATT_HEAD_DIM ** -0.5
        valid = (sel <= q_pos[None, :, None])[:, :, None, :]
        p = jax.nn.softmax(jnp.where(valid, logits, -jnp.inf), axis=-1)
        return jnp.einsum("bqhk,bqkd->bqhd", p.astype(v_sel.dtype), v_sel)

    starts = jnp.arange(nb, dtype=jnp.int32) * Q_BLOCK
    out = lax.map(one_block, (to_blocks(q), to_blocks(q_idx), to_blocks(w_idx), starts))
    return jnp.moveaxis(out, 0, 1).reshape(B, S, ATT_WIDTH)


def mlstm(q, k, v, i_pre, f_pre):
    B, S, NH, d = q.shape
    nc = S // CHUNK

    def chunks(a):
        a = a.astype(jnp.float32).reshape((B, nc, CHUNK, NH) + a.shape[3:])
        return jnp.moveaxis(a, 3, 1)

    qc = chunks(q)
    kc = chunks(k) * d ** -0.5
    vc = chunks(v)
    ic = chunks(i_pre)
    b = jnp.cumsum(jax.nn.log_sigmoid(chunks(f_pre)), axis=-1)
    b_last = b[..., -1]

    a = b_last[..., None] - b + ic
    a_max = a.max(axis=-1)
    wa = jnp.exp(a - a_max[..., None])
    c_chunk = jnp.einsum("bhcl,bhcld,bhcle->bhcde", wa, kc, vc)
    n_chunk = jnp.einsum("bhcl,bhcld->bhcd", wa, kc)

    def step(carry, xs):
        C, n, m = carry
        bl, am, Cc, ncu = xs
        m_new = jnp.maximum(bl + m, am)
        s_old = jnp.exp(bl + m - m_new)
        s_new = jnp.exp(am - m_new)
        C_new = s_old[..., None, None] * C + s_new[..., None, None] * Cc
        n_new = s_old[..., None] * n + s_new[..., None] * ncu
        return (C_new, n_new, m_new), (C, n, m)

    init = (jnp.zeros((B, NH, d, d), jnp.float32), jnp.zeros((B, NH, d), jnp.float32), jnp.zeros((B, NH), jnp.float32))
    xs = (jnp.moveaxis(b_last, 2, 0), jnp.moveaxis(a_max, 2, 0), jnp.moveaxis(c_chunk, 2, 0), jnp.moveaxis(n_chunk, 2, 0))
    _, (C_prev, n_prev, m_prev) = lax.scan(step, init, xs)
    C_prev = jnp.moveaxis(C_prev, 0, 2)
    n_prev = jnp.moveaxis(n_prev, 0, 2)
    m_prev = jnp.moveaxis(m_prev, 0, 2)

    g = b + m_prev[..., None]
    causal = jnp.tril(jnp.ones((CHUNK, CHUNK), dtype=bool))
    dmat = jnp.where(causal, b[..., :, None] - b[..., None, :] + ic[..., None, :], -jnp.inf)
    m_t = jnp.maximum(g, dmat.max(axis=-1))
    w_intra = jnp.exp(dmat - m_t[..., None])
    w_inter = jnp.exp(g - m_t)
    s = jnp.einsum("bhcld,bhcsd->bhcls", qc, kc) * w_intra
    num = w_inter[..., None] * jnp.einsum("bhcld,bhcde->bhcle", qc, C_prev) + jnp.einsum("bhcls,bhcse->bhcle", s, vc)
    den = w_inter * jnp.einsum("bhcld,bhcd->bhcl", qc, n_prev) + s.sum(axis=-1)
    h = num / jnp.maximum(jnp.abs(den), jnp.exp(-m_t))[..., None]
    return jnp.moveaxis(h, 1, 3).reshape(B, S, NH, d)


def hybrid_mixer(h, pos, w_in, conv_w, conv_b, gate_b, q_norm, k_norm, m_norm, w_out):
    B, S, _ = h.shape
    q, k, v, q_idx, k_idx, w_idx, qk_m, v_m, o_m, i_pre, f_pre = split_columns(h @ w_in)
    q = rope(rmsnorm(q.reshape(B, S, ATT_HEADS, ATT_HEAD_DIM), q_norm), pos)
    k = rope(rmsnorm(k, k_norm)[:, :, None, :], pos)[:, :, 0, :]
    q_idx = rope(q_idx.reshape(B, S, IDX_HEADS, IDX_DIM), pos)
    k_idx = rope(k_idx[:, :, None, :], pos)[:, :, 0, :]
    att = dsa_attention(q, k, v, q_idx, k_idx, w_idx * IDX_HEADS ** -0.5)
    qk_m = jax.nn.silu(causal_conv(qk_m, conv_w, conv_b))
    q_m, k_m = jnp.split(qk_m, 2, axis=-1)
    shape_m = (B, S, M_HEADS, M_HEAD_DIM)
    h_m = mlstm(q_m.reshape(shape_m), k_m.reshape(shape_m), v_m.reshape(shape_m),
                i_pre + gate_b[:M_HEADS], f_pre + gate_b[M_HEADS:])
    h_m = rmsnorm(h_m, m_norm).reshape(B, S, M_WIDTH).astype(h.dtype) * jax.nn.sigmoid(o_m)
    return jnp.concatenate([att, h_m], axis=-1) @ w_out


def setup_inputs(seed: int = 0) -> dict:
    key = jax.random.key(seed)
    keys = jax.random.split(key, 19)

    def nrm(k, shape, scale):
        return scale * jax.random.normal(k, shape, jnp.float32)

    mix_w = ATT_WIDTH + M_WIDTH
    return {
        "x": nrm(keys[0], (BATCH, SEQ, D_MODEL), 1.0),
        "ffn1_norm": 1.0 + nrm(keys[1], (DEPTH, D_MODEL), 0.02),
        "ffn1_w_gate": nrm(keys[2], (DEPTH, D_MODEL, D_FF), D_MODEL ** -0.5),
        "ffn1_w_up": nrm(keys[3], (DEPTH, D_MODEL, D_FF), D_MODEL ** -0.5),
        "ffn1_w_down": nrm(keys[4], (DEPTH, D_FF, D_MODEL), D_FF ** -0.5),
        "mix_norm": 1.0 + nrm(keys[5], (DEPTH, D_MODEL), 0.02),
        "w_in": nrm(keys[6], (DEPTH, D_MODEL, IN_COLS), D_MODEL ** -0.5),
        "conv_w": nrm(keys[7], (DEPTH, CONV_K, 1, 2 * M_WIDTH), CONV_K ** -0.5),
        "conv_b": nrm(keys[8], (DEPTH, 2 * M_WIDTH), 0.02),
        "gate_b": jnp.concatenate([nrm(keys[9], (DEPTH, M_HEADS), 0.1),
                                   3.0 + nrm(keys[10], (DEPTH, M_HEADS), 0.5)], axis=-1),
        "q_norm": 1.0 + nrm(keys[11], (DEPTH, ATT_HEAD_DIM), 0.02),
        "k_norm": 1.0 + nrm(keys[12], (DEPTH, ATT_HEAD_DIM), 0.02),
        "m_norm": 1.0 + nrm(keys[13], (DEPTH, M_HEADS, M_HEAD_DIM), 0.02),
        "w_out": nrm(keys[14], (DEPTH, mix_w, D_MODEL), mix_w ** -0.5),
        "ffn2_norm": 1.0 + nrm(keys[15], (DEPTH, D_MODEL), 0.02),
        "ffn2_w_gate": nrm(keys[16], (DEPTH, D_MODEL, D_FF), D_MODEL ** -0.5),
        "ffn2_w_up": nrm(keys[17], (DEPTH, D_MODEL, D_FF), D_MODEL ** -0.5),
        "ffn2_w_down": nrm(keys[18], (DEPTH, D_FF, D_MODEL), D_FF ** -0.5),
    }


def reference(x, ffn1_norm, ffn1_w_gate, ffn1_w_up, ffn1_w_down, mix_norm, w_in, conv_w, conv_b, gate_b,
              q_norm, k_norm, m_norm, w_out, ffn2_norm, ffn2_w_gate, ffn2_w_up, ffn2_w_down):
    pos = jnp.arange(x.shape[1], dtype=jnp.int32)
    for l in range(DEPTH):
        x = x + 0.5 * swiglu(rmsnorm(x, ffn1_norm[l]), ffn1_w_gate[l], ffn1_w_up[l], ffn1_w_down[l])
        x = x + hybrid_mixer(rmsnorm(x, mix_norm[l]), pos, w_in[l], conv_w[l], conv_b[l], gate_b[l],
                             q_norm[l], k_norm[l], m_norm[l], w_out[l])
        x = x + 0.5 * swiglu(rmsnorm(x, ffn2_norm[l]), ffn2_w_gate[l], ffn2_w_up[l], ffn2_w_down[l])
    return x
```

```python
import functools

import jax
import jax.numpy as jnp
from jax import lax
from jax.experimental import pallas as pl
from jax.experimental.pallas import tpu as pltpu

ATT_HEADS = 8
ATT_HEAD_DIM = 64
ATT_WIDTH = ATT_HEADS * ATT_HEAD_DIM
IDX_HEADS = 4
IDX_DIM = 64
TOPK_MAX = 256
M_HEADS = 4
M_HEAD_DIM = 128
M_WIDTH = M_HEADS * M_HEAD_DIM
CHUNK = 128
CONV_K = 4
ROPE_THETA = 10000.0
EPS = 1e-6

LANES = 128
SUBLANES = 8
VMEM_LIMIT_BYTES = 56 * 1024 * 1024

PROJ_COLS = 3072
GATE_BLOCK = 7
W_LANE = 64
I_LANE = 68
F_LANE = 72

INT_MIN = -(2 ** 31)
KEY_NEG_INF = -2139095041
NEG_BIG = -1e30

BF16 = jnp.bfloat16
F32 = jnp.float32

_NT = (((1,), (1,)), ((), ()))
_TN = (((0,), (0,)), ((), ()))


def _params(*sem):
    return pltpu.CompilerParams(dimension_semantics=sem, vmem_limit_bytes=VMEM_LIMIT_BYTES)


def _sigmoid(x):
    return 1.0 / (1.0 + jnp.exp(-x))


def _rms_rows(x, gain):
    return x * lax.rsqrt(jnp.mean(x * x, axis=-1, keepdims=True) + EPS) * gain


def _ffn_kernel(x_ref, g_ref, wg_ref, wu_ref, wd_ref, o_ref, *, chunks):
    x = x_ref[...]
    h = _rms_rows(x, g_ref[...]).astype(BF16)
    acc = jnp.zeros(x.shape, F32)
    f0 = 0
    for fc in chunks:
        g = jnp.dot(h, wg_ref[:, f0:f0 + fc], preferred_element_type=F32)
        u = jnp.dot(h, wu_ref[:, f0:f0 + fc], preferred_element_type=F32)
        a = (g * _sigmoid(g) * u).astype(BF16)
        acc = acc + jnp.dot(a, wd_ref[f0:f0 + fc, :], preferred_element_type=F32)
        f0 += fc
    o_ref[...] = x + 0.5 * acc


def _ffn_chunks(d_ff):
    step = 512
    chunks = [step] * (d_ff // step)
    if d_ff % step:
        chunks.append(d_ff % step)
    return tuple(chunks)


def _ffn(x2, gain, w_gate, w_up, w_down, *, tm=512):
    t, d = x2.shape
    d_ff = w_gate.shape[1]
    const = lambda i: (0, 0)
    return pl.pallas_call(
        functools.partial(_ffn_kernel, chunks=_ffn_chunks(d_ff)),
        grid=(t // tm,),
        in_specs=[
            pl.BlockSpec((tm, d), lambda i: (i, 0)),
            pl.BlockSpec((1, d), const),
            pl.BlockSpec((d, d_ff), const, pipeline_mode=pl.Buffered(1)),
            pl.BlockSpec((d, d_ff), const, pipeline_mode=pl.Buffered(1)),
            pl.BlockSpec((d_ff, d), const, pipeline_mode=pl.Buffered(1)),
        ],
        out_specs=pl.BlockSpec((tm, d), lambda i: (i, 0)),
        out_shape=jax.ShapeDtypeStruct((t, d), F32),
        compiler_params=_params("parallel"),
        name="ffn",
    )(x2, gain.reshape(1, d), w_gate.astype(BF16), w_up.astype(BF16), w_down.astype(BF16))


def _inproj_kernel(x_ref, g_ref, w_ref, o_ref):
    h = _rms_rows(x_ref[...], g_ref[...]).astype(BF16)
    o_ref[...] = jnp.dot(h, w_ref[...], preferred_element_type=F32)


def _inproj(x2, gain, w_arranged, *, tm=512):
    t, d = x2.shape
    const = lambda i: (0, 0)
    return pl.pallas_call(
        _inproj_kernel,
        grid=(t // tm,),
        in_specs=[
            pl.BlockSpec((tm, d), lambda i: (i, 0)),
            pl.BlockSpec((1, d), const),
            pl.BlockSpec((d, PROJ_COLS), const, pipeline_mode=pl.Buffered(1)),
        ],
        out_specs=pl.BlockSpec((tm, PROJ_COLS), lambda i: (i, 0)),
        out_shape=jax.ShapeDtypeStruct((t, PROJ_COLS), F32),
        compiler_params=_params("parallel"),
        name="inproj",
    )(x2, gain.reshape(1, d), w_arranged)


def _arrange_w_in(w_in):
    d = w_in.shape[0]
    head = ATT_WIDTH + 2 * ATT_HEAD_DIM + IDX_HEADS * IDX_DIM + IDX_DIM + IDX_HEADS
    body = 4 * M_WIDTH
    gates = w_in[:, head + body:head + body + 2 * M_HEADS]
    pad = jnp.zeros((d, PROJ_COLS - head - body - 2 * M_HEADS), w_in.dtype)
    return jnp.concatenate([w_in[:, :head], gates, pad, w_in[:, head:head + body]], axis=1).astype(BF16)


def _rope(x, cosf, sinf):
    lane = lax.broadcasted_iota(jnp.int32, x.shape, 1)
    lower = (lane % ATT_HEAD_DIM) < (ATT_HEAD_DIM // 2)
    width = x.shape[1]
    partner = jnp.where(lower, pltpu.roll(x, width - ATT_HEAD_DIM // 2, 1), pltpu.roll(x, ATT_HEAD_DIM // 2, 1))
    return x * cosf + partner * sinf


def _group_mean_sq(x, gmat):
    sq = x * x
    hi = sq.astype(BF16)
    lo = (sq - hi.astype(F32)).astype(BF16)
    tot = jnp.dot(hi, gmat, preferred_element_type=F32) + jnp.dot(lo, gmat, preferred_element_type=F32)
    return tot * (1.0 / ATT_HEAD_DIM)


def _prep_kernel(p_ref, cos_ref, sin_ref, gmat_ref, qn_ref, kn_ref,
                 q_ref, qi_ref, k_ref, v_ref, ki_ref):
    cosf = cos_ref[...]
    sinf = sin_ref[...]
    gmat = gmat_ref[...]
    qk_scale = ATT_HEAD_DIM ** -0.5
    idx_scale = IDX_DIM ** -0.5
    for s in range(ATT_WIDTH // LANES):
        x = p_ref[:, s * LANES:(s + 1) * LANES]
        xn = x * lax.rsqrt(_group_mean_sq(x, gmat) + EPS) * qn_ref[...]
        xr = (_rope(xn, cosf, sinf) * qk_scale).astype(BF16)
        q_ref[2 * s] = xr[:, :ATT_HEAD_DIM]
        q_ref[2 * s + 1] = xr[:, ATT_HEAD_DIM:]
    kv = p_ref[:, ATT_WIDTH:ATT_WIDTH + LANES]
    kn = kv * lax.rsqrt(_group_mean_sq(kv, gmat) + EPS) * kn_ref[...]
    kr = _rope(kn, cosf, sinf)
    k_ref[...] = kr[:, :ATT_HEAD_DIM].astype(BF16)
    v_ref[...] = kv[:, ATT_HEAD_DIM:].astype(BF16)
    base = ATT_WIDTH + LANES
    for s in range(IDX_HEADS * IDX_DIM // LANES):
        x = p_ref[:, base + s * LANES:base + (s + 1) * LANES]
        xr = (_rope(x, cosf, sinf) * idx_scale).astype(BF16)
        qi_ref[2 * s] = xr[:, :IDX_DIM]
        qi_ref[2 * s + 1] = xr[:, IDX_DIM:]
    gk = p_ref[:, GATE_BLOCK * LANES:(GATE_BLOCK + 1) * LANES]
    ki_ref[...] = _rope(gk, cosf, sinf)[:, :IDX_DIM].astype(BF16)


def _rope_tables(seq):
    half = ATT_HEAD_DIM // 2
    inv_freq = ROPE_THETA ** (-jnp.arange(half, dtype=F32) / half)
    ang = jnp.arange(seq, dtype=jnp.int32).astype(F32)[:, None] * inv_freq[None, :]
    cos = jnp.cos(ang)
    sin = jnp.sin(ang)
    cosf = jnp.concatenate([cos, cos, cos, cos], axis=1)
    sinf = jnp.concatenate([-sin, sin, -sin, sin], axis=1)
    return cosf, sinf


def _prep(proj, q_norm, k_norm, *, tm=512):
    b, s, _ = proj.shape
    cosf, sinf = _rope_tables(s)
    lane = jnp.arange(LANES)
    gmat = (lane[:, None] // ATT_HEAD_DIM == lane[None, :] // ATT_HEAD_DIM).astype(BF16)
    qn = jnp.concatenate([q_norm, q_norm]).reshape(1, LANES)
    kn = jnp.concatenate([k_norm, jnp.ones_like(k_norm)]).reshape(1, LANES)
    const = lambda bi, i: (0, 0)
    tok = lambda bi, i: (bi, i, 0)
    head_tok = lambda bi, i: (bi, 0, i, 0)
    return pl.pallas_call(
        _prep_kernel,
        grid=(b, s // tm),
        in_specs=[
            pl.BlockSpec((None, tm, 8 * LANES), tok),
            pl.BlockSpec((tm, LANES), lambda bi, i: (i, 0)),
            pl.BlockSpec((tm, LANES), lambda bi, i: (i, 0)),
            pl.BlockSpec((LANES, LANES), const),
            pl.BlockSpec((1, LANES), const),
            pl.BlockSpec((1, LANES), const),
        ],
        out_specs=[
            pl.BlockSpec((None, ATT_HEADS, tm, ATT_HEAD_DIM), head_tok),
            pl.BlockSpec((None, IDX_HEADS, tm, IDX_DIM), head_tok),
            pl.BlockSpec((None, tm, ATT_HEAD_DIM), tok),
            pl.BlockSpec((None, tm, ATT_HEAD_DIM), tok),
            pl.BlockSpec((None, tm, IDX_DIM), tok),
        ],
        out_shape=[
            jax.ShapeDtypeStruct((b, ATT_HEADS, s, ATT_HEAD_DIM), BF16),
            jax.ShapeDtypeStruct((b, IDX_HEADS, s, IDX_DIM), BF16),
            jax.ShapeDtypeStruct((b, s, ATT_HEAD_DIM), BF16),
            jax.ShapeDtypeStruct((b, s, ATT_HEAD_DIM), BF16),
            jax.ShapeDtypeStruct((b, s, IDX_DIM), BF16),
        ],
        compiler_params=_params("parallel", "parallel"),
        name="dsa_prep",
    )(proj, cosf, sinf, gmat, qn, kn)


def _sortable_key(score):
    score = jnp.where(score == 0.0, 0.0, score)
    bits = lax.bitcast_convert_type(score, jnp.int32)
    return bits ^ ((bits >> 31) & jnp.int32(0x7FFFFFFF))


def _dsa_kernel(q_ref, qi_ref, w_ref, ki_ref, k_ref, v_ref, o_ref,
                key_scr, thr_scr, m_scr, l_scr, acc_scr, *, qblk, topk, rows, idx_bits):
    qb = pl.program_id(1)
    nch = qb + 1
    nrep = qblk // LANES
    row = lax.broadcasted_iota(jnp.int32, (qblk, qblk), 0)
    col = lax.broadcasted_iota(jnp.int32, (qblk, qblk), 1)
    causal = col <= row

    wblk = w_ref[...]
    wcols = [jnp.broadcast_to(wblk[:, W_LANE + h:W_LANE + h + 1] * (IDX_HEADS ** -0.5), (qblk, qblk))
             for h in range(IDX_HEADS)]

    def score_chunk(j, diag):
        ks = pl.multiple_of(j * qblk, qblk)
        kc = ki_ref[pl.ds(ks, qblk), :]
        sc = jnp.zeros((qblk, qblk), F32)
        for h in range(IDX_HEADS):
            logit = lax.dot_general(qi_ref[h], kc, _NT, preferred_element_type=F32)
            sc = sc + jnp.maximum(logit, 0.0) * wcols[h]
        if diag:
            sc = jnp.where(causal, sc, -jnp.inf)
        key_scr[:, pl.ds(ks, qblk)] = _sortable_key(sc)

    def score_body(j, carry):
        score_chunk(j, False)
        return carry

    lax.fori_loop(0, qb, score_body, 0)
    score_chunk(qb, True)

    def count_ge(r0, cand):
        def body(j, acc):
            ks = pl.multiple_of(j * qblk, qblk)
            for c in range(nrep):
                blk = key_scr[r0:r0 + rows, pl.ds(ks + c * LANES, LANES)]
                acc = acc + jnp.where(blk >= cand, 1.0, 0.0)
            return acc
        acc = lax.fori_loop(0, nch, body, jnp.zeros((rows, LANES), F32))
        return jnp.sum(acc, axis=-1, keepdims=True)

    total = (nch * qblk).astype(F32)
    any_tie = jnp.zeros((), F32)
    for r0 in range(0, qblk, rows):
        def bit_body(bi, carry):
            thr, cnt_thr = carry
            cand = thr + lax.shift_left(jnp.int32(1), 31 - bi)
            cnt = jnp.broadcast_to(count_ge(r0, cand), (rows, LANES))
            take = cnt >= topk
            return jnp.where(take, cand, thr), jnp.where(take, cnt, cnt_thr)

        thr0 = jnp.full((rows, LANES), INT_MIN, jnp.int32)
        cnt0 = jnp.broadcast_to(total, (rows, LANES))
        thr, cnt_thr = lax.fori_loop(0, 32, bit_body, (thr0, cnt0))
        thr_scr[r0:r0 + rows, :] = thr
        tie = jnp.where((cnt_thr > topk), 1.0, 0.0) * jnp.where(thr > KEY_NEG_INF, 1.0, 0.0)
        l_scr[0, r0:r0 + rows, :] = cnt_thr
        any_tie = jnp.maximum(any_tie, jnp.max(tie))

    @pl.when(any_tie > 0.0)
    def _():
        for r0 in range(0, qblk, rows):
            thr = thr_scr[r0:r0 + rows, :]
            cnt_thr = l_scr[0, r0:r0 + rows, :]
            tie_row = jnp.where(cnt_thr > topk, 1.0, 0.0) * jnp.where(thr > KEY_NEG_INF, 1.0, 0.0)
            lane = lax.broadcasted_iota(jnp.int32, (rows, LANES), 1)

            def count_eq_below(limit):
                def body(j, acc):
                    ks = pl.multiple_of(j * qblk, qblk)
                    for c in range(nrep):
                        blk = key_scr[r0:r0 + rows, pl.ds(ks + c * LANES, LANES)]
                        if limit is None:
                            hit = jnp.where(blk > thr, 1.0, 0.0)
                        else:
                            idx = lane + (ks + c * LANES)
                            hit = jnp.where(blk == thr, 1.0, 0.0) * jnp.where(idx < limit, 1.0, 0.0)
                        acc = acc + hit
                    return acc
                acc = lax.fori_loop(0, nch, body, jnp.zeros((rows, LANES), F32))
                return jnp.broadcast_to(jnp.sum(acc, axis=-1, keepdims=True), (rows, LANES))

            need = topk - count_eq_below(None)

            def idx_body(bi, pos):
                cand = pos + lax.shift_left(jnp.int32(1), idx_bits - 1 - bi)
                return jnp.where(count_eq_below(cand) < need, cand, pos)

            pos = lax.fori_loop(0, idx_bits, idx_body, jnp.zeros((rows, LANES), jnp.int32))

            def drop_body(j, carry):
                ks = pl.multiple_of(j * qblk, qblk)
                for c in range(nrep):
                    blk = key_scr[r0:r0 + rows, pl.ds(ks + c * LANES, LANES)]
                    idx = lane + (ks + c * LANES)
                    drop = (jnp.where(blk == thr, 1.0, 0.0) * jnp.where(idx > pos, 1.0, 0.0) * tie_row) > 0.0
                    key_scr[r0:r0 + rows, pl.ds(ks + c * LANES, LANES)] = jnp.where(drop, INT_MIN, blk)
                return carry

            lax.fori_loop(0, nch, drop_body, 0)

    m_scr[...] = jnp.full(m_scr.shape, NEG_BIG, F32)
    l_scr[...] = jnp.zeros(l_scr.shape, F32)
    acc_scr[...] = jnp.zeros(acc_scr.shape, F32)
    thr_all = thr_scr[...]

    def attn_chunk(j, diag):
        ks = pl.multiple_of(j * qblk, qblk)
        parts = []
        for c in range(nrep):
            blk = key_scr[:, pl.ds(ks + c * LANES, LANES)]
            parts.append(jnp.where(blk >= thr_all, 0.0, -jnp.inf))
        bias = jnp.concatenate(parts, axis=1)
        if diag:
            bias = jnp.where(causal, bias, -jnp.inf)
        kc = k_ref[pl.ds(ks, qblk), :]
        vc = v_ref[pl.ds(ks, qblk), :]
        for h in range(ATT_HEADS):
            s = lax.dot_general(q_ref[h], kc, _NT, preferred_element_type=F32) + bias
            m_old = m_scr[h]
            m_new = jnp.maximum(m_old, jnp.max(s, axis=-1, keepdims=True))
            alpha = jnp.exp(m_old - m_new)
            p = jnp.exp(s - jnp.concatenate([m_new] * nrep, axis=1))
            l_scr[h] = alpha * l_scr[h] + jnp.sum(p, axis=-1, keepdims=True)
            pv = jnp.dot(p.astype(BF16), vc, preferred_element_type=F32)
            acc_scr[h] = alpha[:, :ATT_HEAD_DIM] * acc_scr[h] + pv
            m_scr[h] = m_new

    def attn_body(j, carry):
        attn_chunk(j, False)
        return carry

    lax.fori_loop(0, qb, attn_body, 0)
    attn_chunk(qb, True)

    outs = [acc_scr[h] / l_scr[h][:, :ATT_HEAD_DIM] for h in range(ATT_HEADS)]
    o_ref[...] = jnp.concatenate(outs, axis=1).astype(o_ref.dtype)


def _dsa(q, qi, proj, ki, k, v, *, qblk=256, rows=128):
    b, _, s, _ = q.shape
    topk = min(TOPK_MAX, s // 4)
    assert qblk >= topk and s % qblk == 0 and qblk % rows == 0
    idx_bits = max(1, (s - 1).bit_length())
    kern = functools.partial(_dsa_kernel, qblk=qblk, topk=float(topk), rows=rows, idx_bits=idx_bits)
    head_tok = lambda bi, i: (bi, 0, i, 0)
    full = lambda bi, i: (bi, 0, 0)
    return pl.pallas_call(
        kern,
        grid=(b, s // qblk),
        in_specs=[
            pl.BlockSpec((None, ATT_HEADS, qblk, ATT_HEAD_DIM), head_tok),
            pl.BlockSpec((None, IDX_HEADS, qblk, IDX_DIM), head_tok),
            pl.BlockSpec((None, qblk, LANES), lambda bi, i: (bi, i, GATE_BLOCK)),
            pl.BlockSpec((None, s, IDX_DIM), full),
            pl.BlockSpec((None, s, ATT_HEAD_DIM), full),
            pl.BlockSpec((None, s, ATT_HEAD_DIM), full),
        ],
        out_specs=pl.BlockSpec((None, qblk, ATT_WIDTH), lambda bi, i: (bi, i, 0)),
        out_shape=jax.ShapeDtypeStruct((b, s, ATT_WIDTH), BF16),
        scratch_shapes=[
            pltpu.VMEM((qblk, s), jnp.int32),
            pltpu.VMEM((qblk, LANES), jnp.int32),
            pltpu.VMEM((ATT_HEADS, qblk, LANES), F32),
            pltpu.VMEM((ATT_HEADS, qblk, LANES), F32),
            pltpu.VMEM((ATT_HEADS, qblk, ATT_HEAD_DIM), F32),
        ],
        compiler_params=_params("parallel", "arbitrary"),
        name="dsa_attention",
    )(q, qi, proj, ki, k, v)


def _log_sigmoid(x):
    return jnp.minimum(x, 0.0) - jnp.log(1.0 + jnp.exp(-jnp.abs(x)))


def _mlstm_kernel(qk_ref, v_ref, og_ref, g_ref, cw_ref, cb_ref, gb_ref, mn_ref, o_ref,
                  xe_scr, c_scr, n_scr, m_scr):
    L = CHUNK
    d = M_HEAD_DIM
    hi = lax.Precision.HIGHEST

    @pl.when(pl.program_id(1) == 0)
    def _():
        xe_scr[0:SUBLANES, :] = jnp.zeros((SUBLANES, 2 * M_WIDTH), F32)
        c_scr[...] = jnp.zeros(c_scr.shape, F32)
        n_scr[...] = jnp.zeros(n_scr.shape, F32)
        m_scr[...] = jnp.zeros(m_scr.shape, F32)

    xe_scr[SUBLANES:SUBLANES + L, :] = qk_ref[...]
    y = jnp.broadcast_to(cb_ref[...], (L, 2 * M_WIDTH))
    for j in range(CONV_K):
        off = SUBLANES - (CONV_K - 1) + j
        y = y + cw_ref[j:j + 1, :] * xe_scr[off:off + L, :]
    xe_scr[0:SUBLANES, :] = xe_scr[L:L + SUBLANES, :]
    qk = y * _sigmoid(y)

    g = g_ref[...] + gb_ref[...]
    ls = _log_sigmoid(g)
    row = lax.broadcasted_iota(jnp.int32, (L, L), 0)
    col = lax.broadcasted_iota(jnp.int32, (L, L), 1)
    lower = row >= col
    tril = jnp.where(lower, 1.0, 0.0)
    triu = jnp.where(row <= col, 1.0, 0.0)
    b_cols = jnp.dot(tril, ls, preferred_element_type=F32, precision=hi)
    g_t = g.T
    b_rows = jnp.dot(ls.T, triu, preferred_element_type=F32, precision=hi)

    for h in range(M_HEADS):
        sl = slice(h * d, (h + 1) * d)
        qh = qk[:, sl]
        kh = qk[:, M_WIDTH + h * d:M_WIDTH + (h + 1) * d] * (d ** -0.5)
        vh = v_ref[:, sl]
        qb16 = qh.astype(BF16)
        kb16 = kh.astype(BF16)
        vb16 = vh.astype(BF16)
        b_col = b_cols[:, F_LANE + h:F_LANE + h + 1]
        i_col = g[:, I_LANE + h:I_LANE + h + 1]
        b_row = b_rows[F_LANE + h:F_LANE + h + 1, :]
        i_row = g_t[I_LANE + h:I_LANE + h + 1, :]
        b_last = b_row[:, L - 1:L]

        c_prev = c_scr[h]
        n_prev = n_scr[h]
        m_prev = m_scr[h]

        a_row = b_last - b_row + i_row
        a_max = jnp.max(a_row, axis=-1, keepdims=True)
        wa_col = jnp.exp(b_last - b_col + i_col - a_max)
        kw = kh * wa_col
        c_chunk = lax.dot_general(kw.astype(BF16), vb16, _TN, preferred_element_type=F32)
        n_chunk = jnp.sum(kw, axis=0, keepdims=True)

        g_col = b_col + m_prev
        dmat = jnp.where(lower, b_col + (i_row - b_row), -jnp.inf)
        m_t = jnp.maximum(g_col, jnp.max(dmat, axis=-1, keepdims=True))
        w_intra = jnp.exp(dmat - m_t)
        w_inter = jnp.exp(g_col - m_t)
        s = lax.dot_general(qb16, kb16, _NT, preferred_element_type=F32) * w_intra
        num = w_inter * jnp.dot(qb16, c_prev.astype(BF16), preferred_element_type=F32) \
            + jnp.dot(s.astype(BF16), vb16, preferred_element_type=F32)
        den = w_inter * jnp.sum(qh * n_prev, axis=-1, keepdims=True) + jnp.sum(s, axis=-1, keepdims=True)
        hh = num / jnp.maximum(jnp.abs(den), jnp.exp(-m_t))
        hn = _rms_rows(hh, mn_ref[:, sl])
        o_ref[:, sl] = (hn * _sigmoid(og_ref[:, sl])).astype(o_ref.dtype)

        m_new = jnp.maximum(b_last + m_prev, a_max)
        s_old = jnp.exp(b_last + m_prev - m_new)
        s_new = jnp.exp(a_max - m_new)
        c_scr[h] = s_old * c_prev + s_new * c_chunk
        n_scr[h] = s_old * n_prev + s_new * n_chunk
        m_scr[h] = m_new


def _mlstm(proj, conv_w, conv_b, gate_b, m_norm):
    b, s, _ = proj.shape
    L = CHUNK
    cw = jnp.zeros((SUBLANES, 2 * M_WIDTH), F32).at[:CONV_K].set(conv_w.reshape(CONV_K, 2 * M_WIDTH))
    gb = jnp.zeros((1, LANES), F32).at[0, I_LANE:I_LANE + 2 * M_HEADS].set(gate_b)
    const = lambda bi, c: (0, 0)
    return pl.pallas_call(
        _mlstm_kernel,
        grid=(b, s // L),
        in_specs=[
            pl.BlockSpec((None, L, 2 * M_WIDTH), lambda bi, c: (bi, c, 1)),
            pl.BlockSpec((None, L, M_WIDTH), lambda bi, c: (bi, c, 4)),
            pl.BlockSpec((None, L, M_WIDTH), lambda bi, c: (bi, c, 5)),
            pl.BlockSpec((None, L, LANES), lambda bi, c: (bi, c, GATE_BLOCK)),
            pl.BlockSpec((SUBLANES, 2 * M_WIDTH), const),
            pl.BlockSpec((1, 2 * M_WIDTH), const),
            pl.BlockSpec((1, LANES), const),
            pl.BlockSpec((1, M_WIDTH), const),
        ],
        out_specs=pl.BlockSpec((None, L, M_WIDTH), lambda bi, c: (bi, c, 0)),
        out_shape=jax.ShapeDtypeStruct((b, s, M_WIDTH), BF16),
        scratch_shapes=[
            pltpu.VMEM((L + SUBLANES, 2 * M_WIDTH), F32),
            pltpu.VMEM((M_HEADS, M_HEAD_DIM, M_HEAD_DIM), F32),
            pltpu.VMEM((M_HEADS, 1, M_HEAD_DIM), F32),
            pltpu.VMEM((M_HEADS, 1, 1), F32),
        ],
        compiler_params=_params("parallel", "arbitrary"),
        name="mlstm",
    )(proj, proj, proj, proj, cw, conv_b.reshape(1, -1), gb, m_norm.reshape(1, M_WIDTH))


def _outproj_kernel(att_ref, hm_ref, x_ref, w_ref, o_ref):
    cat = jnp.concatenate([att_ref[...], hm_ref[...]], axis=1)
    o_ref[...] = x_ref[...] + jnp.dot(cat, w_ref[...], preferred_element_type=F32)


def _outproj(att2, hm2, x2, w_out, *, tm=512):
    t, d = x2.shape
    mix = att2.shape[1] + hm2.shape[1]
    return pl.pallas_call(
        _outproj_kernel,
        grid=(t // tm,),
        in_specs=[
            pl.BlockSpec((tm, att2.shape[1]), lambda i: (i, 0)),
            pl.BlockSpec((tm, hm2.shape[1]), lambda i: (i, 0)),
            pl.BlockSpec((tm, d), lambda i: (i, 0)),
            pl.BlockSpec((mix, d), lambda i: (0, 0), pipeline_mode=pl.Buffered(1)),
        ],
        out_specs=pl.BlockSpec((tm, d), lambda i: (i, 0)),
        out_shape=jax.ShapeDtypeStruct((t, d), F32),
        compiler_params=_params("parallel"),
        name="outproj",
    )(att2, hm2, x2, w_out.astype(BF16))


def kernel(x, ffn1_norm, ffn1_w_gate, ffn1_w_up, ffn1_w_down, mix_norm, w_in, conv_w, conv_b, gate_b,
           q_norm, k_norm, m_norm, w_out, ffn2_norm, ffn2_w_gate, ffn2_w_up, ffn2_w_down):
    b, s, d = x.shape
    depth = w_in.shape[0]
    x2 = x.reshape(b * s, d)
    for l in range(depth):
        x2 = _ffn(x2, ffn1_norm[l], ffn1_w_gate[l], ffn1_w_up[l], ffn1_w_down[l])
        proj = _inproj(x2, mix_norm[l], _arrange_w_in(w_in[l])).reshape(b, s, PROJ_COLS)
        q, qi, k, v, ki = _prep(proj, q_norm[l], k_norm[l])
        att = _dsa(q, qi, proj, ki, k, v)
        hm = _mlstm(proj, conv_w[l], conv_b[l], gate_b[l], m_norm[l])
        x2 = _outproj(att.reshape(b * s, ATT_WIDTH), hm.reshape(b * s, M_WIDTH), x2, w_out[l])
        x2 = _ffn(x2, ffn2_norm[l], ffn2_w_gate[l], ffn2_w_up[l], ffn2_w_down[l])
    return x2.reshape(b, s, d)
```

```python
import functools

import jax
import jax.numpy as jnp
from jax import lax
from jax.experimental import pallas as pl
from jax.experimental.pallas import tpu as pltpu

ATT_HEADS = 8
ATT_HEAD_DIM = 64
ATT_WIDTH = ATT_HEADS * ATT_HEAD_DIM
IDX_HEADS = 4
IDX_DIM = 64
TOPK_MAX = 256
M_HEADS = 4
M_HEAD_DIM = 128
M_WIDTH = M_HEADS * M_HEAD_DIM
CHUNK = 128
CONV_K = 4
ROPE_THETA = 10000.0
EPS = 1e-6

LANES = 128
SUBLANES = 8
MXU_DIM = 256
VMEM_LIMIT_BYTES = 56 * 1024 * 1024

PROJ_COLS = 3072
GATE_BLOCK = 7
W_LANE = 64
I_LANE = 68
F_LANE = 72

I16_MIN = -(2 ** 15)
NEG_BIG = -1e30
LOG2E = 1.4426950408889634

BF16 = jnp.bfloat16
F32 = jnp.float32

_NT = (((1,), (1,)), ((), ()))
_TN = (((0,), (0,)), ((), ()))


def _params(*sem):
    return pltpu.CompilerParams(dimension_semantics=sem, vmem_limit_bytes=VMEM_LIMIT_BYTES)


def _sigmoid(x):
    return 1.0 / (1.0 + jnp.exp(-x))


def _rms_rows(x, gain):
    return x * lax.rsqrt(jnp.mean(x * x, axis=-1, keepdims=True) + EPS) * gain


def _ffn_kernel(x_ref, g_ref, wg_ref, wu_ref, wd_ref, o_ref, *, chunks):
    x = x_ref[...]
    h = _rms_rows(x, g_ref[...]).astype(BF16)
    acc = jnp.zeros(x.shape, F32)
    f0 = 0
    for fc in chunks:
        g = jnp.dot(h, wg_ref[:, f0:f0 + fc], preferred_element_type=F32)
        u = jnp.dot(h, wu_ref[:, f0:f0 + fc], preferred_element_type=F32)
        a = (g * _sigmoid(g) * u).astype(BF16)
        acc = acc + jnp.dot(a, wd_ref[f0:f0 + fc, :], preferred_element_type=F32)
        f0 += fc
    o_ref[...] = x + 0.5 * acc


def _ffn_chunks(d_ff):
    step = 512
    chunks = [step] * (d_ff // step)
    if d_ff % step:
        chunks.append(d_ff % step)
    return tuple(chunks)


def _ffn(x2, gain, w_gate, w_up, w_down, *, tm=512):
    t, d = x2.shape
    d_ff = w_gate.shape[1]
    const = lambda i: (0, 0)
    return pl.pallas_call(
        functools.partial(_ffn_kernel, chunks=_ffn_chunks(d_ff)),
        grid=(t // tm,),
        in_specs=[
            pl.BlockSpec((tm, d), lambda i: (i, 0)),
            pl.BlockSpec((1, d), const),
            pl.BlockSpec((d, d_ff), const, pipeline_mode=pl.Buffered(1)),
            pl.BlockSpec((d, d_ff), const, pipeline_mode=pl.Buffered(1)),
            pl.BlockSpec((d_ff, d), const, pipeline_mode=pl.Buffered(1)),
        ],
        out_specs=pl.BlockSpec((tm, d), lambda i: (i, 0)),
        out_shape=jax.ShapeDtypeStruct((t, d), F32),
        compiler_params=_params("parallel"),
        name="ffn",
    )(x2, gain.reshape(1, d), w_gate.astype(BF16), w_up.astype(BF16), w_down.astype(BF16))


def _inproj_kernel(x_ref, g_ref, w_ref, o_ref):
    h = _rms_rows(x_ref[...], g_ref[...]).astype(BF16)
    o_ref[...] = jnp.dot(h, w_ref[...], preferred_element_type=F32)


def _inproj(x2, gain, w_arranged, *, tm=512):
    t, d = x2.shape
    const = lambda i: (0, 0)
    return pl.pallas_call(
        _inproj_kernel,
        grid=(t // tm,),
        in_specs=[
            pl.BlockSpec((tm, d), lambda i: (i, 0)),
            pl.BlockSpec((1, d), const),
            pl.BlockSpec((d, PROJ_COLS), const, pipeline_mode=pl.Buffered(1)),
        ],
        out_specs=pl.BlockSpec((tm, PROJ_COLS), lambda i: (i, 0)),
        out_shape=jax.ShapeDtypeStruct((t, PROJ_COLS), F32),
        compiler_params=_params("parallel"),
        name="inproj",
    )(x2, gain.reshape(1, d), w_arranged)


def _arrange_w_in(w_in):
    d = w_in.shape[0]
    head = ATT_WIDTH + 2 * ATT_HEAD_DIM + IDX_HEADS * IDX_DIM + IDX_DIM + IDX_HEADS
    body = 4 * M_WIDTH
    gates = w_in[:, head + body:head + body + 2 * M_HEADS]
    pad = jnp.zeros((d, PROJ_COLS - head - body - 2 * M_HEADS), w_in.dtype)
    return jnp.concatenate([w_in[:, :head], gates, pad, w_in[:, head:head + body]], axis=1).astype(BF16)


def _rope(x, cosf, sinf):
    lane = lax.broadcasted_iota(jnp.int32, x.shape, 1)
    lower = (lane % ATT_HEAD_DIM) < (ATT_HEAD_DIM // 2)
    width = x.shape[1]
    partner = jnp.where(lower, pltpu.roll(x, width - ATT_HEAD_DIM // 2, 1), pltpu.roll(x, ATT_HEAD_DIM // 2, 1))
    return x * cosf + partner * sinf


def _group_mean_sq(x, gmat):
    sq = x * x
    hi = sq.astype(BF16)
    lo = (sq - hi.astype(F32)).astype(BF16)
    tot = jnp.dot(hi, gmat, preferred_element_type=F32) + jnp.dot(lo, gmat, preferred_element_type=F32)
    return tot * (1.0 / ATT_HEAD_DIM)


def _prep_kernel(p_ref, cos_ref, sin_ref, gmat_ref, qn_ref, kn_ref,
                 q_ref, qi_ref, k_ref, v_ref, ki_ref):
    cosf = cos_ref[...]
    sinf = sin_ref[...]
    gmat = gmat_ref[...]
    qk_scale = ATT_HEAD_DIM ** -0.5 * LOG2E
    idx_scale = IDX_DIM ** -0.5
    for s in range(ATT_WIDTH // LANES):
        x = p_ref[:, s * LANES:(s + 1) * LANES]
        xn = x * lax.rsqrt(_group_mean_sq(x, gmat) + EPS) * qn_ref[...]
        xr = (_rope(xn, cosf, sinf) * qk_scale).astype(BF16)
        q_ref[2 * s] = xr[:, :ATT_HEAD_DIM]
        q_ref[2 * s + 1] = xr[:, ATT_HEAD_DIM:]
    kv = p_ref[:, ATT_WIDTH:ATT_WIDTH + LANES]
    kn = kv * lax.rsqrt(_group_mean_sq(kv, gmat) + EPS) * kn_ref[...]
    kr = _rope(kn, cosf, sinf)
    k_ref[...] = kr[:, :ATT_HEAD_DIM].astype(BF16)
    lane = lax.broadcasted_iota(jnp.int32, kv.shape, 1)
    v_ref[...] = jnp.where(lane < ATT_HEAD_DIM, pltpu.roll(kv, ATT_HEAD_DIM, 1), 1.0).astype(BF16)
    base = ATT_WIDTH + LANES
    for s in range(IDX_HEADS * IDX_DIM // LANES):
        x = p_ref[:, base + s * LANES:base + (s + 1) * LANES]
        xr = (_rope(x, cosf, sinf) * idx_scale).astype(BF16)
        qi_ref[2 * s] = xr[:, :IDX_DIM]
        qi_ref[2 * s + 1] = xr[:, IDX_DIM:]
    gk = p_ref[:, GATE_BLOCK * LANES:(GATE_BLOCK + 1) * LANES]
    ki_ref[...] = _rope(gk, cosf, sinf)[:, :IDX_DIM].astype(BF16)


def _rope_tables(seq):
    half = ATT_HEAD_DIM // 2
    inv_freq = ROPE_THETA ** (-jnp.arange(half, dtype=F32) / half)
    ang = jnp.arange(seq, dtype=jnp.int32).astype(F32)[:, None] * inv_freq[None, :]
    cos = jnp.cos(ang)
    sin = jnp.sin(ang)
    cosf = jnp.concatenate([cos, cos, cos, cos], axis=1)
    sinf = jnp.concatenate([-sin, sin, -sin, sin], axis=1)
    return cosf, sinf


def _prep(proj, q_norm, k_norm, *, tm=512):
    b, s, _ = proj.shape
    cosf, sinf = _rope_tables(s)
    lane = jnp.arange(LANES)
    gmat = (lane[:, None] // ATT_HEAD_DIM == lane[None, :] // ATT_HEAD_DIM).astype(BF16)
    qn = jnp.concatenate([q_norm, q_norm]).reshape(1, LANES)
    kn = jnp.concatenate([k_norm, jnp.ones_like(k_norm)]).reshape(1, LANES)
    const = lambda bi, i: (0, 0)
    tok = lambda bi, i: (bi, i, 0)
    head_tok = lambda bi, i: (bi, 0, i, 0)
    return pl.pallas_call(
        _prep_kernel,
        grid=(b, s // tm),
        in_specs=[
            pl.BlockSpec((None, tm, 8 * LANES), tok),
            pl.BlockSpec((tm, LANES), lambda bi, i: (i, 0)),
            pl.BlockSpec((tm, LANES), lambda bi, i: (i, 0)),
            pl.BlockSpec((LANES, LANES), const),
            pl.BlockSpec((1, LANES), const),
            pl.BlockSpec((1, LANES), const),
        ],
        out_specs=[
            pl.BlockSpec((None, ATT_HEADS, tm, ATT_HEAD_DIM), head_tok),
            pl.BlockSpec((None, IDX_HEADS, tm, IDX_DIM), head_tok),
            pl.BlockSpec((None, tm, ATT_HEAD_DIM), tok),
            pl.BlockSpec((None, tm, LANES), tok),
            pl.BlockSpec((None, tm, IDX_DIM), tok),
        ],
        out_shape=[
            jax.ShapeDtypeStruct((b, ATT_HEADS, s, ATT_HEAD_DIM), BF16),
            jax.ShapeDtypeStruct((b, IDX_HEADS, s, IDX_DIM), BF16),
            jax.ShapeDtypeStruct((b, s, ATT_HEAD_DIM), BF16),
            jax.ShapeDtypeStruct((b, s, LANES), BF16),
            jax.ShapeDtypeStruct((b, s, IDX_DIM), BF16),
        ],
        compiler_params=_params("parallel", "parallel"),
        name="dsa_prep",
    )(proj, cosf, sinf, gmat, qn, kn)


def _sortable_key(score):
    score = jnp.where(score == 0.0, 0.0, score)
    bits = lax.bitcast_convert_type(score, jnp.int32)
    return bits ^ ((bits >> 31) & jnp.int32(0x7FFFFFFF))


def _dsa_kernel(q_ref, qi_ref, w_ref, ki_ref, k_ref, v_ref, o_ref,
                hi_scr, lo_scr, thr_scr, need_scr, m_scr, acc_scr, *, qblk, topk, rows):
    qb = pl.program_id(1)
    nch = qb + 1
    nrep = qblk // LANES
    row = lax.broadcasted_iota(jnp.int32, (qblk, qblk), 0)
    col = lax.broadcasted_iota(jnp.int32, (qblk, qblk), 1)
    causal = col <= row

    wblk = w_ref[...]
    wcols = [jnp.broadcast_to(wblk[:, W_LANE + h:W_LANE + h + 1] * (IDX_HEADS ** -0.5), (qblk, qblk))
             for h in range(IDX_HEADS)]

    def score_chunk(j, diag):
        ks = pl.multiple_of(j * qblk, qblk)
        kc = ki_ref[pl.ds(ks, qblk), :]
        sc = jnp.zeros((qblk, qblk), F32)
        for h in range(IDX_HEADS):
            logit = lax.dot_general(qi_ref[h], kc, _NT, preferred_element_type=F32)
            sc = sc + jnp.maximum(logit, 0.0) * wcols[h]
        if diag:
            sc = jnp.where(causal, sc, -jnp.inf)
        key = _sortable_key(sc)
        hi_scr[:, pl.ds(ks, qblk)] = (key >> 16).astype(jnp.int16)
        lo_scr[:, pl.ds(ks, qblk)] = ((key & 0xFFFF) + I16_MIN).astype(jnp.int16)

    def score_body(j, carry):
        score_chunk(j, False)
        return carry

    lax.fori_loop(0, qb, score_body, 0)
    score_chunk(qb, True)

    one16 = jnp.ones((), BF16)
    zero16 = jnp.zeros((), BF16)
    lane_ones = jnp.ones((LANES, LANES), BF16)

    def count16(src_scr, r0, cand, strict):
        def slab(acc, ks, c):
            blk = src_scr[r0:r0 + rows, pl.ds(ks + c * LANES, LANES)]
            hit = (blk > cand) if strict else (blk >= cand)
            return acc + jnp.where(hit, one16, zero16)

        def pair_body(j, acc):
            ks = pl.multiple_of(j * (2 * qblk), 2 * qblk)
            for c in range(2 * nrep):
                acc = slab(acc, ks, c)
            return acc

        def single_body(j, acc):
            ks = pl.multiple_of(j * qblk, qblk)
            for c in range(nrep):
                acc = slab(acc, ks, c)
            return acc

        npair = nch // 2
        acc = lax.fori_loop(0, npair, pair_body, jnp.zeros((rows, LANES), BF16))
        acc = lax.fori_loop(2 * npair, nch, single_body, acc)
        return jnp.dot(acc, lane_ones, preferred_element_type=F32)

    def bisect16(src_scr, r0, target):
        def body(bi, thr):
            cand = thr + lax.shift_left(jnp.int32(1), 15 - bi)
            cnt = count16(src_scr, r0, cand.astype(jnp.int16), False)
            return jnp.where(cnt >= target, cand, thr)
        return lax.fori_loop(0, 16, body, jnp.full((rows, LANES), I16_MIN, jnp.int32))

    for r0 in range(0, qblk, rows):
        thr_hi = bisect16(hi_scr, r0, topk)
        thr_hi16 = thr_hi.astype(jnp.int16)
        quota = topk - count16(hi_scr, r0, thr_hi16, True)

        def pin_body(j, carry):
            ks = pl.multiple_of(j * qblk, qblk)
            for c in range(nrep):
                sl = (slice(r0, r0 + rows), pl.ds(ks + c * LANES, LANES))
                lo_scr[sl] = jnp.where(hi_scr[sl] == thr_hi16, lo_scr[sl], jnp.int16(I16_MIN))
            return carry

        lax.fori_loop(0, nch, pin_body, 0)
        thr_lo = bisect16(lo_scr, r0, quota)
        thr_lo16 = thr_lo.astype(jnp.int16)
        thr_scr[0, r0:r0 + rows, :] = thr_hi16
        thr_scr[1, r0:r0 + rows, :] = thr_lo16
        need_scr[r0:r0 + rows, :] = quota - count16(lo_scr, r0, thr_lo16, True)

    m_scr[...] = jnp.full(m_scr.shape, NEG_BIG, F32)
    acc_scr[...] = jnp.zeros(acc_scr.shape, F32)
    thr_hi_all = thr_scr[0]
    thr_lo_all = thr_scr[1]
    need_all = need_scr[...]
    nsub = MXU_DIM // LANES
    prefix_ones = jnp.where(lax.broadcasted_iota(jnp.int32, (MXU_DIM, MXU_DIM), 0)
                            <= lax.broadcasted_iota(jnp.int32, (MXU_DIM, MXU_DIM), 1), 1.0, 0.0).astype(BF16)
    ninf16 = jnp.full((), -jnp.inf, BF16)

    def attn_chunk(j, seen, diag):
        ks = pl.multiple_of(j * qblk, qblk)
        above, equal = [], []
        for c in range(nrep):
            hi_c = hi_scr[:, pl.ds(ks + c * LANES, LANES)]
            lo_c = lo_scr[:, pl.ds(ks + c * LANES, LANES)]
            above.append(jnp.where(hi_c > thr_hi_all, zero16, jnp.where(lo_c > thr_lo_all, zero16, ninf16)))
            equal.append(jnp.where(hi_c == thr_hi_all, jnp.where(lo_c == thr_lo_all, one16, zero16), zero16))
        parts = []
        for u in range(qblk // MXU_DIM):
            eq_u = jnp.concatenate(equal[u * nsub:(u + 1) * nsub], axis=1)
            rank = jnp.dot(eq_u, prefix_ones, preferred_element_type=F32)
            room = jnp.concatenate([need_all - seen] * nsub, axis=1)
            fits = jnp.where(rank <= room, 0.0, -jnp.inf)
            above_u = jnp.concatenate(above[u * nsub:(u + 1) * nsub], axis=1).astype(F32)
            parts.append(jnp.where(eq_u.astype(F32) > 0.5, fits, above_u))
            seen = seen + jnp.max(rank, axis=-1, keepdims=True)
        bias = jnp.concatenate(parts, axis=1)
        if diag:
            bias = jnp.where(causal, bias, -jnp.inf)
        kc = k_ref[pl.ds(ks, qblk), :]
        vc = v_ref[pl.ds(ks, qblk), :]
        for h in range(ATT_HEADS):
            s = lax.dot_general(q_ref[h], kc, _NT, preferred_element_type=F32) + bias
            m_old = m_scr[h]
            m_new = jnp.maximum(m_old, jnp.max(s, axis=-1, keepdims=True))
            alpha = jnp.exp2(m_old - m_new)
            p = jnp.exp2(s - jnp.concatenate([m_new] * nrep, axis=1))
            acc_scr[h] = alpha * acc_scr[h] + jnp.dot(p.astype(BF16), vc, preferred_element_type=F32)
            m_scr[h] = m_new
        return seen

    seen = lax.fori_loop(0, qb, lambda j, seen: attn_chunk(j, seen, False), jnp.zeros((qblk, LANES), F32))
    attn_chunk(qb, seen, True)

    outs = []
    for h in range(ATT_HEADS):
        acc = acc_scr[h]
        outs.append(acc[:, :ATT_HEAD_DIM] / acc[:, ATT_HEAD_DIM:])
    o_ref[...] = jnp.concatenate(outs, axis=1).astype(o_ref.dtype)


def _dsa(q, qi, proj, ki, k, v, *, qblk=512, rows=256):
    b, _, s, _ = q.shape
    topk = min(TOPK_MAX, s // 4)
    assert qblk >= topk and s % qblk == 0 and qblk % rows == 0 and qblk % MXU_DIM == 0
    assert s // LANES <= 256
    kern = functools.partial(_dsa_kernel, qblk=qblk, topk=float(topk), rows=rows)
    head_tok = lambda bi, i: (bi, 0, i, 0)
    full = lambda bi, i: (bi, 0, 0)
    once = pl.Buffered(1)
    return pl.pallas_call(
        kern,
        grid=(b, s // qblk),
        in_specs=[
            pl.BlockSpec((None, ATT_HEADS, qblk, ATT_HEAD_DIM), head_tok),
            pl.BlockSpec((None, IDX_HEADS, qblk, IDX_DIM), head_tok),
            pl.BlockSpec((None, qblk, LANES), lambda bi, i: (bi, i, GATE_BLOCK)),
            pl.BlockSpec((None, s, IDX_DIM), full, pipeline_mode=once),
            pl.BlockSpec((None, s, ATT_HEAD_DIM), full, pipeline_mode=once),
            pl.BlockSpec((None, s, LANES), full, pipeline_mode=once),
        ],
        out_specs=pl.BlockSpec((None, qblk, ATT_WIDTH), lambda bi, i: (bi, i, 0)),
        out_shape=jax.ShapeDtypeStruct((b, s, ATT_WIDTH), BF16),
        scratch_shapes=[
            pltpu.VMEM((qblk, s), jnp.int16),
            pltpu.VMEM((qblk, s), jnp.int16),
            pltpu.VMEM((2, qblk, LANES), jnp.int16),
            pltpu.VMEM((qblk, LANES), F32),
            pltpu.VMEM((ATT_HEADS, qblk, LANES), F32),
            pltpu.VMEM((ATT_HEADS, qblk, LANES), F32),
        ],
        compiler_params=_params("parallel", "arbitrary"),
        name="dsa_attention",
    )(q, qi, proj, ki, k, v)


def _log_sigmoid(x):
    return jnp.minimum(x, 0.0) - jnp.log(1.0 + jnp.exp(-jnp.abs(x)))


def _mlstm_kernel(qk_ref, v_ref, og_ref, g_ref, cw_ref, cb_ref, gb_ref, mn_ref, o_ref,
                  xe_scr, c_scr, n_scr, m_scr):
    L = CHUNK
    d = M_HEAD_DIM
    hi = lax.Precision.HIGHEST

    @pl.when(pl.program_id(1) == 0)
    def _():
        xe_scr[0:SUBLANES, :] = jnp.zeros((SUBLANES, 2 * M_WIDTH), F32)
        c_scr[...] = jnp.zeros(c_scr.shape, F32)
        n_scr[...] = jnp.zeros(n_scr.shape, F32)
        m_scr[...] = jnp.zeros(m_scr.shape, F32)

    xe_scr[SUBLANES:SUBLANES + L, :] = qk_ref[...]
    y = jnp.broadcast_to(cb_ref[...], (L, 2 * M_WIDTH))
    for j in range(CONV_K):
        off = SUBLANES - (CONV_K - 1) + j
        y = y + cw_ref[j:j + 1, :] * xe_scr[off:off + L, :]
    xe_scr[0:SUBLANES, :] = xe_scr[L:L + SUBLANES, :]
    qk = y * _sigmoid(y)

    g = g_ref[...] + gb_ref[...]
    ls = _log_sigmoid(g)
    row = lax.broadcasted_iota(jnp.int32, (L, L), 0)
    col = lax.broadcasted_iota(jnp.int32, (L, L), 1)
    lower = row >= col
    tril = jnp.where(lower, 1.0, 0.0)
    triu = jnp.where(row <= col, 1.0, 0.0)
    b_cols = jnp.dot(tril, ls, preferred_element_type=F32, precision=hi)
    g_t = g.T
    b_rows = jnp.dot(ls.T, triu, preferred_element_type=F32, precision=hi)

    for h in range(M_HEADS):
        sl = slice(h * d, (h + 1) * d)
        qh = qk[:, sl]
        kh = qk[:, M_WIDTH + h * d:M_WIDTH + (h + 1) * d] * (d ** -0.5)
        vh = v_ref[:, sl]
        qb16 = qh.astype(BF16)
        kb16 = kh.astype(BF16)
        vb16 = vh.astype(BF16)
        b_col = b_cols[:, F_LANE + h:F_LANE + h + 1]
        i_col = g[:, I_LANE + h:I_LANE + h + 1]
        b_row = b_rows[F_LANE + h:F_LANE + h + 1, :]
        i_row = g_t[I_LANE + h:I_LANE + h + 1, :]
        b_last = b_row[:, L - 1:L]

        c_prev = c_scr[h]
        n_prev = n_scr[h]
        m_prev = m_scr[h]

        a_row = b_last - b_row + i_row
        a_max = jnp.max(a_row, axis=-1, keepdims=True)
        wa_col = jnp.exp(b_last - b_col + i_col - a_max)
        kw = kh * wa_col
        c_chunk = lax.dot_general(kw.astype(BF16), vb16, _TN, preferred_element_type=F32)
        n_chunk = jnp.sum(kw, axis=0, keepdims=True)

        g_col = b_col + m_prev
        dmat = jnp.where(lower, b_col + (i_row - b_row), -jnp.inf)
        m_t = jnp.maximum(g_col, jnp.max(dmat, axis=-1, keepdims=True))
        w_intra = jnp.exp(dmat - m_t)
        w_inter = jnp.exp(g_col - m_t)
        s = lax.dot_general(qb16, kb16, _NT, preferred_element_type=F32) * w_intra
        num = w_inter * jnp.dot(qb16, c_prev.astype(BF16), preferred_element_type=F32) \
            + jnp.dot(s.astype(BF16), vb16, preferred_element_type=F32)
        den = w_inter * jnp.sum(qh * n_prev, axis=-1, keepdims=True) + jnp.sum(s, axis=-1, keepdims=True)
        hh = num / jnp.maximum(jnp.abs(den), jnp.exp(-m_t))
        hn = _rms_rows(hh, mn_ref[:, sl])
        o_ref[:, sl] = (hn * _sigmoid(og_ref[:, sl])).astype(o_ref.dtype)

        m_new = jnp.maximum(b_last + m_prev, a_max)
        s_old = jnp.exp(b_last + m_prev - m_new)
        s_new = jnp.exp(a_max - m_new)
        c_scr[h] = s_old * c_prev + s_new * c_chunk
        n_scr[h] = s_old * n_prev + s_new * n_chunk
        m_scr[h] = m_new


def _mlstm(proj, conv_w, conv_b, gate_b, m_norm):
    b, s, _ = proj.shape
    L = CHUNK
    cw = jnp.zeros((SUBLANES, 2 * M_WIDTH), F32).at[:CONV_K].set(conv_w.reshape(CONV_K, 2 * M_WIDTH))
    gb = jnp.zeros((1, LANES), F32).at[0, I_LANE:I_LANE + 2 * M_HEADS].set(gate_b)
    const = lambda bi, c: (0, 0)
    return pl.pallas_call(
        _mlstm_kernel,
        grid=(b, s // L),
        in_specs=[
            pl.BlockSpec((None, L, 2 * M_WIDTH), lambda bi, c: (bi, c, 1)),
            pl.BlockSpec((None, L, M_WIDTH), lambda bi, c: (bi, c, 4)),
            pl.BlockSpec((None, L, M_WIDTH), lambda bi, c: (bi, c, 5)),
            pl.BlockSpec((None, L, LANES), lambda bi, c: (bi, c, GATE_BLOCK)),
            pl.BlockSpec((SUBLANES, 2 * M_WIDTH), const),
            pl.BlockSpec((1, 2 * M_WIDTH), const),
            pl.BlockSpec((1, LANES), const),
            pl.BlockSpec((1, M_WIDTH), const),
        ],
        out_specs=pl.BlockSpec((None, L, M_WIDTH), lambda bi, c: (bi, c, 0)),
        out_shape=jax.ShapeDtypeStruct((b, s, M_WIDTH), BF16),
        scratch_shapes=[
            pltpu.VMEM((L + SUBLANES, 2 * M_WIDTH), F32),
            pltpu.VMEM((M_HEADS, M_HEAD_DIM, M_HEAD_DIM), F32),
            pltpu.VMEM((M_HEADS, 1, M_HEAD_DIM), F32),
            pltpu.VMEM((M_HEADS, 1, 1), F32),
        ],
        compiler_params=_params("parallel", "arbitrary"),
        name="mlstm",
    )(proj, proj, proj, proj, cw, conv_b.reshape(1, -1), gb, m_norm.reshape(1, M_WIDTH))


def _outproj_kernel(att_ref, hm_ref, x_ref, w_ref, o_ref):
    cat = jnp.concatenate([att_ref[...], hm_ref[...]], axis=1)
    o_ref[...] = x_ref[...] + jnp.dot(cat, w_ref[...], preferred_element_type=F32)


def _outproj(att2, hm2, x2, w_out, *, tm=512):
    t, d = x2.shape
    mix = att2.shape[1] + hm2.shape[1]
    return pl.pallas_call(
        _outproj_kernel,
        grid=(t // tm,),
        in_specs=[
            pl.BlockSpec((tm, att2.shape[1]), lambda i: (i, 0)),
            pl.BlockSpec((tm, hm2.shape[1]), lambda i: (i, 0)),
            pl.BlockSpec((tm, d), lambda i: (i, 0)),
            pl.BlockSpec((mix, d), lambda i: (0, 0), pipeline_mode=pl.Buffered(1)),
        ],
        out_specs=pl.BlockSpec((tm, d), lambda i: (i, 0)),
        out_shape=jax.ShapeDtypeStruct((t, d), F32),
        compiler_params=_params("parallel"),
        name="outproj",
    )(att2, hm2, x2, w_out.astype(BF16))


def kernel(x, ffn1_norm, ffn1_w_gate, ffn1_w_up, ffn1_w_down, mix_norm, w_in, conv_w, conv_b, gate_b,
           q_norm, k_norm, m_norm, w_out, ffn2_norm, ffn2_w_gate, ffn2_w_up, ffn2_w_down):
    b, s, d = x.shape
    depth = w_in.shape[0]
    x2 = x.reshape(b * s, d)
    for l in range(depth):
        x2 = _ffn(x2, ffn1_norm[l], ffn1_w_gate[l], ffn1_w_up[l], ffn1_w_down[l])
        proj = _inproj(x2, mix_norm[l], _arrange_w_in(w_in[l])).reshape(b, s, PROJ_COLS)
        q, qi, k, v, ki = _prep(proj, q_norm[l], k_norm[l])
        att = _dsa(q, qi, proj, ki, k, v)
        hm = _mlstm(proj, conv_w[l], conv_b[l], gate_b[l], m_norm[l])
        x2 = _outproj(att.reshape(b * s, ATT_WIDTH), hm.reshape(b * s, M_WIDTH), x2, w_out[l])
        x2 = _ffn(x2, ffn2_norm[l], ffn2_w_gate[l], ffn2_w_up[l], ffn2_w_down[l])
    return x2.reshape(b, s, d)
```

```python
import functools

import jax
import jax.numpy as jnp
from jax import lax
from jax.experimental import pallas as pl
from jax.experimental.pallas import tpu as pltpu

ATT_HEADS = 8
ATT_HEAD_DIM = 64
ATT_WIDTH = ATT_HEADS * ATT_HEAD_DIM
IDX_HEADS = 4
IDX_DIM = 64
TOPK_MAX = 256
M_HEADS = 4
M_HEAD_DIM = 128
M_WIDTH = M_HEADS * M_HEAD_DIM
CHUNK = 128
CONV_K = 4
ROPE_THETA = 10000.0
EPS = 1e-6

LANES = 128
SUBLANES = 8
MXU_DIM = 256
VMEM_LIMIT_BYTES = 56 * 1024 * 1024

PROJ_COLS = 3072
GATE_BLOCK = 7
W_LANE = 64
I_LANE = 68
F_LANE = 72

I16_MIN = -(2 ** 15)
NEG_BIG = -1e30
LOG2E = 1.4426950408889634

BF16 = jnp.bfloat16
F32 = jnp.float32

_NT = (((1,), (1,)), ((), ()))
_TN = (((0,), (0,)), ((), ()))


def _params(*sem):
    return pltpu.CompilerParams(dimension_semantics=sem, vmem_limit_bytes=VMEM_LIMIT_BYTES)


def _sigmoid(x):
    return 1.0 / (1.0 + jnp.exp(-x))


def _rms_rows(x, gain):
    return x * lax.rsqrt(jnp.mean(x * x, axis=-1, keepdims=True) + EPS) * gain


def _ffn_kernel(x_ref, g_ref, wg_ref, wu_ref, wd_ref, o_ref, *, chunks):
    x = x_ref[...]
    h = _rms_rows(x, g_ref[...]).astype(BF16)
    acc = jnp.zeros(x.shape, F32)
    f0 = 0
    for fc in chunks:
        g = jnp.dot(h, wg_ref[:, f0:f0 + fc], preferred_element_type=F32)
        u = jnp.dot(h, wu_ref[:, f0:f0 + fc], preferred_element_type=F32)
        a = (g * _sigmoid(g) * u).astype(BF16)
        acc = acc + jnp.dot(a, wd_ref[f0:f0 + fc, :], preferred_element_type=F32)
        f0 += fc
    o_ref[...] = x + 0.5 * acc


def _ffn_chunks(d_ff):
    step = 512
    chunks = [step] * (d_ff // step)
    if d_ff % step:
        chunks.append(d_ff % step)
    return tuple(chunks)


def _ffn(x2, gain, w_gate, w_up, w_down, *, tm=512):
    t, d = x2.shape
    d_ff = w_gate.shape[1]
    const = lambda i: (0, 0)
    return pl.pallas_call(
        functools.partial(_ffn_kernel, chunks=_ffn_chunks(d_ff)),
        grid=(t // tm,),
        in_specs=[
            pl.BlockSpec((tm, d), lambda i: (i, 0)),
            pl.BlockSpec((1, d), const),
            pl.BlockSpec((d, d_ff), const, pipeline_mode=pl.Buffered(1)),
            pl.BlockSpec((d, d_ff), const, pipeline_mode=pl.Buffered(1)),
            pl.BlockSpec((d_ff, d), const, pipeline_mode=pl.Buffered(1)),
        ],
        out_specs=pl.BlockSpec((tm, d), lambda i: (i, 0)),
        out_shape=jax.ShapeDtypeStruct((t, d), F32),
        compiler_params=_params("parallel"),
        name="ffn",
    )(x2, gain.reshape(1, d), w_gate.astype(BF16), w_up.astype(BF16), w_down.astype(BF16))


def _inproj_kernel(x_ref, g_ref, w_ref, o_ref):
    h = _rms_rows(x_ref[...], g_ref[...]).astype(BF16)
    o_ref[...] = jnp.dot(h, w_ref[...], preferred_element_type=F32)


def _inproj(x2, gain, w_arranged, *, tm=512):
    t, d = x2.shape
    const = lambda i: (0, 0)
    return pl.pallas_call(
        _inproj_kernel,
        grid=(t // tm,),
        in_specs=[
            pl.BlockSpec((tm, d), lambda i: (i, 0)),
            pl.BlockSpec((1, d), const),
            pl.BlockSpec((d, PROJ_COLS), const, pipeline_mode=pl.Buffered(1)),
        ],
        out_specs=pl.BlockSpec((tm, PROJ_COLS), lambda i: (i, 0)),
        out_shape=jax.ShapeDtypeStruct((t, PROJ_COLS), F32),
        compiler_params=_params("parallel"),
        name="inproj",
    )(x2, gain.reshape(1, d), w_arranged)


def _arrange_w_in(w_in):
    d = w_in.shape[0]
    head = ATT_WIDTH + 2 * ATT_HEAD_DIM + IDX_HEADS * IDX_DIM + IDX_DIM + IDX_HEADS
    body = 4 * M_WIDTH
    gates = w_in[:, head + body:head + body + 2 * M_HEADS]
    pad = jnp.zeros((d, PROJ_COLS - head - body - 2 * M_HEADS), w_in.dtype)
    return jnp.concatenate([w_in[:, :head], gates, pad, w_in[:, head:head + body]], axis=1).astype(BF16)


def _rope(x, cosf, sinf):
    lane = lax.broadcasted_iota(jnp.int32, x.shape, 1)
    lower = (lane % ATT_HEAD_DIM) < (ATT_HEAD_DIM // 2)
    width = x.shape[1]
    partner = jnp.where(lower, pltpu.roll(x, width - ATT_HEAD_DIM // 2, 1), pltpu.roll(x, ATT_HEAD_DIM // 2, 1))
    return x * cosf + partner * sinf


def _group_mean_sq(x, gmat):
    sq = x * x
    hi = sq.astype(BF16)
    lo = (sq - hi.astype(F32)).astype(BF16)
    tot = jnp.dot(hi, gmat, preferred_element_type=F32) + jnp.dot(lo, gmat, preferred_element_type=F32)
    return tot * (1.0 / ATT_HEAD_DIM)


def _prep_kernel(p_ref, cos_ref, sin_ref, gmat_ref, qn_ref, kn_ref,
                 q_ref, qi_ref, k_ref, v_ref, ki_ref):
    cosf = cos_ref[...]
    sinf = sin_ref[...]
    gmat = gmat_ref[...]
    qk_scale = ATT_HEAD_DIM ** -0.5 * LOG2E
    idx_scale = IDX_DIM ** -0.5
    for s in range(ATT_WIDTH // LANES):
        x = p_ref[:, s * LANES:(s + 1) * LANES]
        xn = x * lax.rsqrt(_group_mean_sq(x, gmat) + EPS) * qn_ref[...]
        xr = (_rope(xn, cosf, sinf) * qk_scale).astype(BF16)
        q_ref[2 * s] = xr[:, :ATT_HEAD_DIM]
        q_ref[2 * s + 1] = xr[:, ATT_HEAD_DIM:]
    kv = p_ref[:, ATT_WIDTH:ATT_WIDTH + LANES]
    kn = kv * lax.rsqrt(_group_mean_sq(kv, gmat) + EPS) * kn_ref[...]
    kr = _rope(kn, cosf, sinf)
    k_ref[...] = kr[:, :ATT_HEAD_DIM].astype(BF16)
    lane = lax.broadcasted_iota(jnp.int32, kv.shape, 1)
    v_ref[...] = jnp.where(lane < ATT_HEAD_DIM, pltpu.roll(kv, ATT_HEAD_DIM, 1), 1.0).astype(BF16)
    base = ATT_WIDTH + LANES
    for s in range(IDX_HEADS * IDX_DIM // LANES):
        x = p_ref[:, base + s * LANES:base + (s + 1) * LANES]
        xr = (_rope(x, cosf, sinf) * idx_scale).astype(BF16)
        qi_ref[2 * s] = xr[:, :IDX_DIM]
        qi_ref[2 * s + 1] = xr[:, IDX_DIM:]
    gk = p_ref[:, GATE_BLOCK * LANES:(GATE_BLOCK + 1) * LANES]
    ki_ref[...] = _rope(gk, cosf, sinf)[:, :IDX_DIM].astype(BF16)


def _rope_tables(seq):
    half = ATT_HEAD_DIM // 2
    inv_freq = ROPE_THETA ** (-jnp.arange(half, dtype=F32) / half)
    ang = jnp.arange(seq, dtype=jnp.int32).astype(F32)[:, None] * inv_freq[None, :]
    cos = jnp.cos(ang)
    sin = jnp.sin(ang)
    cosf = jnp.concatenate([cos, cos, cos, cos], axis=1)
    sinf = jnp.concatenate([-sin, sin, -sin, sin], axis=1)
    return cosf, sinf


def _prep(proj, q_norm, k_norm, *, tm=512):
    b, s, _ = proj.shape
    cosf, sinf = _rope_tables(s)
    lane = jnp.arange(LANES)
    gmat = (lane[:, None] // ATT_HEAD_DIM == lane[None, :] // ATT_HEAD_DIM).astype(BF16)
    qn = jnp.concatenate([q_norm, q_norm]).reshape(1, LANES)
    kn = jnp.concatenate([k_norm, jnp.ones_like(k_norm)]).reshape(1, LANES)
    const = lambda bi, i: (0, 0)
    tok = lambda bi, i: (bi, i, 0)
    head_tok = lambda bi, i: (bi, 0, i, 0)
    return pl.pallas_call(
        _prep_kernel,
        grid=(b, s // tm),
        in_specs=[
            pl.BlockSpec((None, tm, 8 * LANES), tok),
            pl.BlockSpec((tm, LANES), lambda bi, i: (i, 0)),
            pl.BlockSpec((tm, LANES), lambda bi, i: (i, 0)),
            pl.BlockSpec((LANES, LANES), const),
            pl.BlockSpec((1, LANES), const),
            pl.BlockSpec((1, LANES), const),
        ],
        out_specs=[
            pl.BlockSpec((None, ATT_HEADS, tm, ATT_HEAD_DIM), head_tok),
            pl.BlockSpec((None, IDX_HEADS, tm, IDX_DIM), head_tok),
            pl.BlockSpec((None, tm, ATT_HEAD_DIM), tok),
            pl.BlockSpec((None, tm, LANES), tok),
            pl.BlockSpec((None, tm, IDX_DIM), tok),
        ],
        out_shape=[
            jax.ShapeDtypeStruct((b, ATT_HEADS, s, ATT_HEAD_DIM), BF16),
            jax.ShapeDtypeStruct((b, IDX_HEADS, s, IDX_DIM), BF16),
            jax.ShapeDtypeStruct((b, s, ATT_HEAD_DIM), BF16),
            jax.ShapeDtypeStruct((b, s, LANES), BF16),
            jax.ShapeDtypeStruct((b, s, IDX_DIM), BF16),
        ],
        compiler_params=_params("parallel", "parallel"),
        name="dsa_prep",
    )(proj, cosf, sinf, gmat, qn, kn)


def _sortable_key(score):
    score = jnp.where(score == 0.0, 0.0, score)
    bits = lax.bitcast_convert_type(score, jnp.int32)
    return bits ^ ((bits >> 31) & jnp.int32(0x7FFFFFFF))


def _dsa_kernel(q_ref, qi_ref, w_ref, ki_ref, k_ref, v_ref, o_ref,
                hi_scr, lo_scr, thr_scr, need_scr, m_scr, acc_scr, *, qblk, topk, rows):
    qb = pl.program_id(1)
    nch = qb + 1
    nrep = qblk // LANES
    row = lax.broadcasted_iota(jnp.int32, (qblk, qblk), 0)
    col = lax.broadcasted_iota(jnp.int32, (qblk, qblk), 1)
    causal = col <= row

    wblk = w_ref[...]
    wcols = [jnp.broadcast_to(wblk[:, W_LANE + h:W_LANE + h + 1] * (IDX_HEADS ** -0.5), (qblk, qblk))
             for h in range(IDX_HEADS)]

    def score_chunk(j, diag):
        ks = pl.multiple_of(j * qblk, qblk)
        kc = ki_ref[pl.ds(ks, qblk), :]
        sc = jnp.zeros((qblk, qblk), F32)
        for h in range(IDX_HEADS):
            logit = lax.dot_general(qi_ref[h], kc, _NT, preferred_element_type=F32)
            sc = sc + jnp.maximum(logit, 0.0) * wcols[h]
        if diag:
            sc = jnp.where(causal, sc, -jnp.inf)
        key = _sortable_key(sc)
        hi_scr[:, pl.ds(ks, qblk)] = (key >> 16).astype(jnp.int16)
        lo_scr[:, pl.ds(ks, qblk)] = ((key & 0xFFFF) + I16_MIN).astype(jnp.int16)

    def score_body(j, carry):
        score_chunk(j, False)
        return carry

    lax.fori_loop(0, qb, score_body, 0)
    score_chunk(qb, True)

    one16 = jnp.ones((), BF16)
    zero16 = jnp.zeros((), BF16)
    lane_ones = jnp.ones((LANES, LANES), BF16)

    groups = list(range(0, qblk, rows))

    def hits16(src_scr, r0, cand):
        def slab(acc, ks, c):
            blk = src_scr[r0:r0 + rows, pl.ds(ks + c * LANES, LANES)]
            return acc + jnp.where(blk >= cand, one16, zero16)

        def pair_body(j, acc):
            ks = pl.multiple_of(j * (2 * qblk), 2 * qblk)
            for c in range(2 * nrep):
                acc = slab(acc, ks, c)
            return acc

        def single_body(j, acc):
            ks = pl.multiple_of(j * qblk, qblk)
            for c in range(nrep):
                acc = slab(acc, ks, c)
            return acc

        npair = nch // 2
        acc = lax.fori_loop(0, npair, pair_body, jnp.zeros((rows, LANES), BF16))
        return lax.fori_loop(2 * npair, nch, single_body, acc)

    def counts16(src_scr, cands):
        accs = [hits16(src_scr, r0, cand) for r0, cand in zip(groups, cands)]
        return [jnp.dot(acc, lane_ones, preferred_element_type=F32) for acc in accs]

    def bisect16(src_scr, targets):
        def body(bi, carry):
            thrs, aboves = carry
            step = lax.shift_left(jnp.int32(1), 15 - bi)
            cands = [thr + step for thr in thrs]
            cnts = counts16(src_scr, [cand.astype(jnp.int16) for cand in cands])
            takes = [cnt >= target for cnt, target in zip(cnts, targets)]
            return (tuple(jnp.where(take, cand, thr) for take, cand, thr in zip(takes, cands, thrs)),
                    tuple(jnp.where(take, above, cnt) for take, above, cnt in zip(takes, aboves, cnts)))
        start = (tuple(jnp.full((rows, LANES), I16_MIN, jnp.int32) for _ in groups),
                 tuple(jnp.zeros((rows, LANES), F32) for _ in groups))
        thrs, aboves = lax.fori_loop(0, 16, body, start)
        return [thr.astype(jnp.int16) for thr in thrs], aboves

    thr_hi16, above_hi = bisect16(hi_scr, [topk] * len(groups))
    quotas = [topk - cnt for cnt in above_hi]

    def pin_body(j, carry):
        ks = pl.multiple_of(j * qblk, qblk)
        for r0, thr16 in zip(groups, thr_hi16):
            for c in range(nrep):
                sl = (slice(r0, r0 + rows), pl.ds(ks + c * LANES, LANES))
                lo_scr[sl] = jnp.where(hi_scr[sl] == thr16, lo_scr[sl], jnp.int16(I16_MIN))
        return carry

    lax.fori_loop(0, nch, pin_body, 0)
    thr_lo16, above_lo = bisect16(lo_scr, quotas)
    needs = [quota - cnt for quota, cnt in zip(quotas, above_lo)]
    for r0, hi16, lo16, need in zip(groups, thr_hi16, thr_lo16, needs):
        thr_scr[0, r0:r0 + rows, :] = hi16
        thr_scr[1, r0:r0 + rows, :] = lo16
        need_scr[r0:r0 + rows, :] = need

    m_scr[...] = jnp.full(m_scr.shape, NEG_BIG, F32)
    acc_scr[...] = jnp.zeros(acc_scr.shape, F32)
    thr_hi_all = thr_scr[0]
    thr_lo_all = thr_scr[1]
    need_all = need_scr[...]
    nsub = MXU_DIM // LANES
    prefix_ones = jnp.where(lax.broadcasted_iota(jnp.int32, (MXU_DIM, MXU_DIM), 0)
                            <= lax.broadcasted_iota(jnp.int32, (MXU_DIM, MXU_DIM), 1), 1.0, 0.0).astype(BF16)
    ninf16 = jnp.full((), -jnp.inf, BF16)

    def attn_chunk(j, seen, diag):
        ks = pl.multiple_of(j * qblk, qblk)
        above, equal = [], []
        for c in range(nrep):
            hi_c = hi_scr[:, pl.ds(ks + c * LANES, LANES)]
            lo_c = lo_scr[:, pl.ds(ks + c * LANES, LANES)]
            above.append(jnp.where(hi_c > thr_hi_all, zero16, jnp.where(lo_c > thr_lo_all, zero16, ninf16)))
            equal.append(jnp.where(hi_c == thr_hi_all, jnp.where(lo_c == thr_lo_all, one16, zero16), zero16))
        parts = []
        for u in range(qblk // MXU_DIM):
            eq_u = jnp.concatenate(equal[u * nsub:(u + 1) * nsub], axis=1)
            rank = jnp.dot(eq_u, prefix_ones, preferred_element_type=F32)
            room = jnp.concatenate([need_all - seen] * nsub, axis=1)
            fits = jnp.where(rank <= room, 0.0, -jnp.inf)
            above_u = jnp.concatenate(above[u * nsub:(u + 1) * nsub], axis=1).astype(F32)
            parts.append(jnp.where(eq_u.astype(F32) > 0.5, fits, above_u))
            seen = seen + jnp.max(rank, axis=-1, keepdims=True)
        bias = jnp.concatenate(parts, axis=1)
        if diag:
            bias = jnp.where(causal, bias, -jnp.inf)
        kc = k_ref[pl.ds(ks, qblk), :]
        vc = v_ref[pl.ds(ks, qblk), :]
        for h in range(ATT_HEADS):
            s = lax.dot_general(q_ref[h], kc, _NT, preferred_element_type=F32) + bias
            m_old = m_scr[h]
            m_new = jnp.maximum(m_old, jnp.max(s, axis=-1, keepdims=True))
            alpha = jnp.exp2(m_old - m_new)
            p = jnp.exp2(s - jnp.concatenate([m_new] * nrep, axis=1))
            acc_scr[h] = alpha * acc_scr[h] + jnp.dot(p.astype(BF16), vc, preferred_element_type=F32)
            m_scr[h] = m_new
        return seen

    seen = lax.fori_loop(0, qb, lambda j, seen: attn_chunk(j, seen, False), jnp.zeros((qblk, LANES), F32))
    attn_chunk(qb, seen, True)

    outs = []
    for h in range(ATT_HEADS):
        acc = acc_scr[h]
        outs.append(acc[:, :ATT_HEAD_DIM] / acc[:, ATT_HEAD_DIM:])
    o_ref[...] = jnp.concatenate(outs, axis=1).astype(o_ref.dtype)


def _dsa(q, qi, proj, ki, k, v, *, qblk=512, rows=256):
    b, _, s, _ = q.shape
    topk = min(TOPK_MAX, s // 4)
    assert qblk >= topk and s % qblk == 0 and qblk % rows == 0 and qblk % MXU_DIM == 0
    assert s // LANES <= 256
    kern = functools.partial(_dsa_kernel, qblk=qblk, topk=float(topk), rows=rows)
    head_tok = lambda bi, i: (bi, 0, i, 0)
    full = lambda bi, i: (bi, 0, 0)
    once = pl.Buffered(1)
    return pl.pallas_call(
        kern,
        grid=(b, s // qblk),
        in_specs=[
            pl.BlockSpec((None, ATT_HEADS, qblk, ATT_HEAD_DIM), head_tok),
            pl.BlockSpec((None, IDX_HEADS, qblk, IDX_DIM), head_tok),
            pl.BlockSpec((None, qblk, LANES), lambda bi, i: (bi, i, GATE_BLOCK)),
            pl.BlockSpec((None, s, IDX_DIM), full, pipeline_mode=once),
            pl.BlockSpec((None, s, ATT_HEAD_DIM), full, pipeline_mode=once),
            pl.BlockSpec((None, s, LANES), full, pipeline_mode=once),
        ],
        out_specs=pl.BlockSpec((None, qblk, ATT_WIDTH), lambda bi, i: (bi, i, 0)),
        out_shape=jax.ShapeDtypeStruct((b, s, ATT_WIDTH), BF16),
        scratch_shapes=[
            pltpu.VMEM((qblk, s + LANES), jnp.int16),
            pltpu.VMEM((qblk, s + LANES), jnp.int16),
            pltpu.VMEM((2, qblk, LANES), jnp.int16),
            pltpu.VMEM((qblk, LANES), F32),
            pltpu.VMEM((ATT_HEADS, qblk, LANES), F32),
            pltpu.VMEM((ATT_HEADS, qblk, LANES), F32),
        ],
        compiler_params=_params("parallel", "arbitrary"),
        name="dsa_attention",
    )(q, qi, proj, ki, k, v)


def _log_sigmoid(x):
    return jnp.minimum(x, 0.0) - jnp.log(1.0 + jnp.exp(-jnp.abs(x)))


def _mlstm_kernel(qk_ref, v_ref, og_ref, g_ref, cw_ref, cb_ref, gb_ref, mn_ref, o_ref,
                  xe_scr, c_scr, n_scr, m_scr):
    L = CHUNK
    d = M_HEAD_DIM
    hi = lax.Precision.HIGHEST

    @pl.when(pl.program_id(1) == 0)
    def _():
        xe_scr[0:SUBLANES, :] = jnp.zeros((SUBLANES, 2 * M_WIDTH), F32)
        c_scr[...] = jnp.zeros(c_scr.shape, F32)
        n_scr[...] = jnp.zeros(n_scr.shape, F32)
        m_scr[...] = jnp.zeros(m_scr.shape, F32)

    xe_scr[SUBLANES:SUBLANES + L, :] = qk_ref[...]
    y = jnp.broadcast_to(cb_ref[...], (L, 2 * M_WIDTH))
    for j in range(CONV_K):
        off = SUBLANES - (CONV_K - 1) + j
        y = y + cw_ref[j:j + 1, :] * xe_scr[off:off + L, :]
    xe_scr[0:SUBLANES, :] = xe_scr[L:L + SUBLANES, :]
    qk = y * _sigmoid(y)

    g = g_ref[...] + gb_ref[...]
    ls = _log_sigmoid(g)
    row = lax.broadcasted_iota(jnp.int32, (L, L), 0)
    col = lax.broadcasted_iota(jnp.int32, (L, L), 1)
    lower = row >= col
    tril = jnp.where(lower, 1.0, 0.0)
    triu = jnp.where(row <= col, 1.0, 0.0)
    b_cols = jnp.dot(tril, ls, preferred_element_type=F32, precision=hi)
    g_t = g.T
    b_rows = jnp.dot(ls.T, triu, preferred_element_type=F32, precision=hi)

    for h in range(M_HEADS):
        sl = slice(h * d, (h + 1) * d)
        qh = qk[:, sl]
        kh = qk[:, M_WIDTH + h * d:M_WIDTH + (h + 1) * d] * (d ** -0.5)
        vh = v_ref[:, sl]
        qb16 = qh.astype(BF16)
        kb16 = kh.astype(BF16)
        vb16 = vh.astype(BF16)
        b_col = b_cols[:, F_LANE + h:F_LANE + h + 1]
        i_col = g[:, I_LANE + h:I_LANE + h + 1]
        b_row = b_rows[F_LANE + h:F_LANE + h + 1, :]
        i_row = g_t[I_LANE + h:I_LANE + h + 1, :]
        b_last = b_row[:, L - 1:L]

        c_prev = c_scr[h]
        n_prev = n_scr[h]
        m_prev = m_scr[h]

        a_row = b_last - b_row + i_row
        a_max = jnp.max(a_row, axis=-1, keepdims=True)
        wa_col = jnp.exp(b_last - b_col + i_col - a_max)
        kw = kh * wa_col
        c_chunk = lax.dot_general(kw.astype(BF16), vb16, _TN, preferred_element_type=F32)
        n_chunk = jnp.sum(kw, axis=0, keepdims=True)

        g_col = b_col + m_prev
        dmat = jnp.where(lower, b_col + (i_row - b_row), -jnp.inf)
        m_t = jnp.maximum(g_col, jnp.max(dmat, axis=-1, keepdims=True))
        w_intra = jnp.exp(dmat - m_t)
        w_inter = jnp.exp(g_col - m_t)
        s = lax.dot_general(qb16, kb16, _NT, preferred_element_type=F32) * w_intra
        num = w_inter * jnp.dot(qb16, c_prev.astype(BF16), preferred_element_type=F32) \
            + jnp.dot(s.astype(BF16), vb16, preferred_element_type=F32)
        den = w_inter * jnp.sum(qh * n_prev, axis=-1, keepdims=True) + jnp.sum(s, axis=-1, keepdims=True)
        hh = num / jnp.maximum(jnp.abs(den), jnp.exp(-m_t))
        hn = _rms_rows(hh, mn_ref[:, sl])
        o_ref[:, sl] = (hn * _sigmoid(og_ref[:, sl])).astype(o_ref.dtype)

        m_new = jnp.maximum(b_last + m_prev, a_max)
        s_old = jnp.exp(b_last + m_prev - m_new)
        s_new = jnp.exp(a_max - m_new)
        c_scr[h] = s_old * c_prev + s_new * c_chunk
        n_scr[h] = s_old * n_prev + s_new * n_chunk
        m_scr[h] = m_new


def _mlstm(proj, conv_w, conv_b, gate_b, m_norm):
    b, s, _ = proj.shape
    L = CHUNK
    cw = jnp.zeros((SUBLANES, 2 * M_WIDTH), F32).at[:CONV_K].set(conv_w.reshape(CONV_K, 2 * M_WIDTH))
    gb = jnp.zeros((1, LANES), F32).at[0, I_LANE:I_LANE + 2 * M_HEADS].set(gate_b)
    const = lambda bi, c: (0, 0)
    return pl.pallas_call(
        _mlstm_kernel,
        grid=(b, s // L),
        in_specs=[
            pl.BlockSpec((None, L, 2 * M_WIDTH), lambda bi, c: (bi, c, 1)),
            pl.BlockSpec((None, L, M_WIDTH), lambda bi, c: (bi, c, 4)),
            pl.BlockSpec((None, L, M_WIDTH), lambda bi, c: (bi, c, 5)),
            pl.BlockSpec((None, L, LANES), lambda bi, c: (bi, c, GATE_BLOCK)),
            pl.BlockSpec((SUBLANES, 2 * M_WIDTH), const),
            pl.BlockSpec((1, 2 * M_WIDTH), const),
            pl.BlockSpec((1, LANES), const),
            pl.BlockSpec((1, M_WIDTH), const),
        ],
        out_specs=pl.BlockSpec((None, L, M_WIDTH), lambda bi, c: (bi, c, 0)),
        out_shape=jax.ShapeDtypeStruct((b, s, M_WIDTH), BF16),
        scratch_shapes=[
            pltpu.VMEM((L + SUBLANES, 2 * M_WIDTH), F32),
            pltpu.VMEM((M_HEADS, M_HEAD_DIM, M_HEAD_DIM), F32),
            pltpu.VMEM((M_HEADS, 1, M_HEAD_DIM), F32),
            pltpu.VMEM((M_HEADS, 1, 1), F32),
        ],
        compiler_params=_params("parallel", "arbitrary"),
        name="mlstm",
    )(proj, proj, proj, proj, cw, conv_b.reshape(1, -1), gb, m_norm.reshape(1, M_WIDTH))


def _outproj_kernel(att_ref, hm_ref, x_ref, w_ref, o_ref):
    cat = jnp.concatenate([att_ref[...], hm_ref[...]], axis=1)
    o_ref[...] = x_ref[...] + jnp.dot(cat, w_ref[...], preferred_element_type=F32)


def _outproj(att2, hm2, x2, w_out, *, tm=512):
    t, d = x2.shape
    mix = att2.shape[1] + hm2.shape[1]
    return pl.pallas_call(
        _outproj_kernel,
        grid=(t // tm,),
        in_specs=[
            pl.BlockSpec((tm, att2.shape[1]), lambda i: (i, 0)),
            pl.BlockSpec((tm, hm2.shape[1]), lambda i: (i, 0)),
            pl.BlockSpec((tm, d), lambda i: (i, 0)),
            pl.BlockSpec((mix, d), lambda i: (0, 0), pipeline_mode=pl.Buffered(1)),
        ],
        out_specs=pl.BlockSpec((tm, d), lambda i: (i, 0)),
        out_shape=jax.ShapeDtypeStruct((t, d), F32),
        compiler_params=_params("parallel"),
        name="outproj",
    )(att2, hm2, x2, w_out.astype(BF16))


def kernel(x, ffn1_norm, ffn1_w_gate, ffn1_w_up, ffn1_w_down, mix_norm, w_in, conv_w, conv_b, gate_b,
           q_norm, k_norm, m_norm, w_out, ffn2_norm, ffn2_w_gate, ffn2_w_up, ffn2_w_down):
    b, s, d = x.shape
    depth = w_in.shape[0]
    x2 = x.reshape(b * s, d)
    for l in range(depth):
        x2 = _ffn(x2, ffn1_norm[l], ffn1_w_gate[l], ffn1_w_up[l], ffn1_w_down[l])
        proj = _inproj(x2, mix_norm[l], _arrange_w_in(w_in[l])).reshape(b, s, PROJ_COLS)
        q, qi, k, v, ki = _prep(proj, q_norm[l], k_norm[l])
        att = _dsa(q, qi, proj, ki, k, v)
        hm = _mlstm(proj, conv_w[l], conv_b[l], gate_b[l], m_norm[l])
        x2 = _outproj(att.reshape(b * s, ATT_WIDTH), hm.reshape(b * s, M_WIDTH), x2, w_out[l])
        x2 = _ffn(x2, ffn2_norm[l], ffn2_w_gate[l], ffn2_w_up[l], ffn2_w_down[l])
    return x2.reshape(b, s, d)
```

```python
import functools

import jax
import jax.numpy as jnp
from jax import lax
from jax.experimental import pallas as pl
from jax.experimental.pallas import tpu as pltpu

ATT_HEADS = 8
ATT_HEAD_DIM = 64
ATT_WIDTH = ATT_HEADS * ATT_HEAD_DIM
IDX_HEADS = 4
IDX_DIM = 64
TOPK_MAX = 256
M_HEADS = 4
M_HEAD_DIM = 128
M_WIDTH = M_HEADS * M_HEAD_DIM
CHUNK = 128
CONV_K = 4
ROPE_THETA = 10000.0
EPS = 1e-6

LANES = 128
SUBLANES = 8
MXU_DIM = 256
VMEM_LIMIT_BYTES = 56 * 1024 * 1024

PROJ_COLS = 3072
GATE_BLOCK = 7
W_LANE = 64
I_LANE = 68
F_LANE = 72

I16_MIN = -(2 ** 15)
NEG_BIG = -1e30
LOG2E = 1.4426950408889634

BF16 = jnp.bfloat16
F32 = jnp.float32

_NT = (((1,), (1,)), ((), ()))
_TN = (((0,), (0,)), ((), ()))


def _params(*sem):
    return pltpu.CompilerParams(dimension_semantics=sem, vmem_limit_bytes=VMEM_LIMIT_BYTES)


def _sigmoid(x):
    return 1.0 / (1.0 + jnp.exp(-x))


def _rms_rows(x, gain):
    return x * lax.rsqrt(jnp.mean(x * x, axis=-1, keepdims=True) + EPS) * gain


def _ffn_kernel(x_ref, g_ref, wg_ref, wu_ref, wd_ref, o_ref, *, chunks):
    x = x_ref[...]
    h = _rms_rows(x, g_ref[...]).astype(BF16)
    acc = jnp.zeros(x.shape, F32)
    f0 = 0
    for fc in chunks:
        g = jnp.dot(h, wg_ref[:, f0:f0 + fc], preferred_element_type=F32)
        u = jnp.dot(h, wu_ref[:, f0:f0 + fc], preferred_element_type=F32)
        a = (g * _sigmoid(g) * u).astype(BF16)
        acc = acc + jnp.dot(a, wd_ref[f0:f0 + fc, :], preferred_element_type=F32)
        f0 += fc
    o_ref[...] = x + 0.5 * acc


def _ffn_chunks(d_ff):
    step = 512
    chunks = [step] * (d_ff // step)
    if d_ff % step:
        chunks.append(d_ff % step)
    return tuple(chunks)


def _ffn(x2, gain, w_gate, w_up, w_down, *, tm=512):
    t, d = x2.shape
    d_ff = w_gate.shape[1]
    const = lambda i: (0, 0)
    return pl.pallas_call(
        functools.partial(_ffn_kernel, chunks=_ffn_chunks(d_ff)),
        grid=(t // tm,),
        in_specs=[
            pl.BlockSpec((tm, d), lambda i: (i, 0)),
            pl.BlockSpec((1, d), const),
            pl.BlockSpec((d, d_ff), const, pipeline_mode=pl.Buffered(1)),
            pl.BlockSpec((d, d_ff), const, pipeline_mode=pl.Buffered(1)),
            pl.BlockSpec((d_ff, d), const, pipeline_mode=pl.Buffered(1)),
        ],
        out_specs=pl.BlockSpec((tm, d), lambda i: (i, 0)),
        out_shape=jax.ShapeDtypeStruct((t, d), F32),
        compiler_params=_params("parallel"),
        name="ffn",
    )(x2, gain.reshape(1, d), w_gate.astype(BF16), w_up.astype(BF16), w_down.astype(BF16))


def _inproj_kernel(x_ref, g_ref, w_ref, o_ref):
    h = _rms_rows(x_ref[...], g_ref[...]).astype(BF16)
    o_ref[...] = jnp.dot(h, w_ref[...], preferred_element_type=F32)


def _inproj(x2, gain, w_arranged, *, tm=512):
    t, d = x2.shape
    const = lambda i: (0, 0)
    return pl.pallas_call(
        _inproj_kernel,
        grid=(t // tm,),
        in_specs=[
            pl.BlockSpec((tm, d), lambda i: (i, 0)),
            pl.BlockSpec((1, d), const),
            pl.BlockSpec((d, PROJ_COLS), const, pipeline_mode=pl.Buffered(1)),
        ],
        out_specs=pl.BlockSpec((tm, PROJ_COLS), lambda i: (i, 0)),
        out_shape=jax.ShapeDtypeStruct((t, PROJ_COLS), F32),
        compiler_params=_params("parallel"),
        name="inproj",
    )(x2, gain.reshape(1, d), w_arranged)


def _arrange_w_in(w_in):
    d = w_in.shape[0]
    head = ATT_WIDTH + 2 * ATT_HEAD_DIM + IDX_HEADS * IDX_DIM + IDX_DIM + IDX_HEADS
    body = 4 * M_WIDTH
    gates = w_in[:, head + body:head + body + 2 * M_HEADS]
    pad = jnp.zeros((d, PROJ_COLS - head - body - 2 * M_HEADS), w_in.dtype)
    return jnp.concatenate([w_in[:, :head], gates, pad, w_in[:, head:head + body]], axis=1).astype(BF16)


def _rope(x, cosf, sinf):
    lane = lax.broadcasted_iota(jnp.int32, x.shape, 1)
    lower = (lane % ATT_HEAD_DIM) < (ATT_HEAD_DIM // 2)
    width = x.shape[1]
    partner = jnp.where(lower, pltpu.roll(x, width - ATT_HEAD_DIM // 2, 1), pltpu.roll(x, ATT_HEAD_DIM // 2, 1))
    return x * cosf + partner * sinf


def _group_mean_sq(x, gmat):
    sq = x * x
    hi = sq.astype(BF16)
    lo = (sq - hi.astype(F32)).astype(BF16)
    tot = jnp.dot(hi, gmat, preferred_element_type=F32) + jnp.dot(lo, gmat, preferred_element_type=F32)
    return tot * (1.0 / ATT_HEAD_DIM)


def _prep_kernel(p_ref, cos_ref, sin_ref, gmat_ref, qn_ref, kn_ref,
                 q_ref, qi_ref, k_ref, v_ref, ki_ref):
    cosf = cos_ref[...]
    sinf = sin_ref[...]
    gmat = gmat_ref[...]
    qk_scale = ATT_HEAD_DIM ** -0.5 * LOG2E
    idx_scale = IDX_DIM ** -0.5
    for s in range(ATT_WIDTH // LANES):
        x = p_ref[:, s * LANES:(s + 1) * LANES]
        xn = x * lax.rsqrt(_group_mean_sq(x, gmat) + EPS) * qn_ref[...]
        xr = (_rope(xn, cosf, sinf) * qk_scale).astype(BF16)
        q_ref[2 * s] = xr[:, :ATT_HEAD_DIM]
        q_ref[2 * s + 1] = xr[:, ATT_HEAD_DIM:]
    kv = p_ref[:, ATT_WIDTH:ATT_WIDTH + LANES]
    kn = kv * lax.rsqrt(_group_mean_sq(kv, gmat) + EPS) * kn_ref[...]
    kr = _rope(kn, cosf, sinf)
    k_ref[...] = kr[:, :ATT_HEAD_DIM].astype(BF16)
    lane = lax.broadcasted_iota(jnp.int32, kv.shape, 1)
    v_ref[...] = jnp.where(lane < ATT_HEAD_DIM, pltpu.roll(kv, ATT_HEAD_DIM, 1), 1.0).astype(BF16)
    base = ATT_WIDTH + LANES
    for s in range(IDX_HEADS * IDX_DIM // LANES):
        x = p_ref[:, base + s * LANES:base + (s + 1) * LANES]
        xr = (_rope(x, cosf, sinf) * idx_scale).astype(BF16)
        qi_ref[2 * s] = xr[:, :IDX_DIM]
        qi_ref[2 * s + 1] = xr[:, IDX_DIM:]
    gk = p_ref[:, GATE_BLOCK * LANES:(GATE_BLOCK + 1) * LANES]
    ki_ref[...] = _rope(gk, cosf, sinf)[:, :IDX_DIM].astype(BF16)


def _rope_tables(seq):
    half = ATT_HEAD_DIM // 2
    inv_freq = ROPE_THETA ** (-jnp.arange(half, dtype=F32) / half)
    ang = jnp.arange(seq, dtype=jnp.int32).astype(F32)[:, None] * inv_freq[None, :]
    cos = jnp.cos(ang)
    sin = jnp.sin(ang)
    cosf = jnp.concatenate([cos, cos, cos, cos], axis=1)
    sinf = jnp.concatenate([-sin, sin, -sin, sin], axis=1)
    return cosf, sinf


def _prep(proj, q_norm, k_norm, *, tm=512):
    b, s, _ = proj.shape
    cosf, sinf = _rope_tables(s)
    lane = jnp.arange(LANES)
    gmat = (lane[:, None] // ATT_HEAD_DIM == lane[None, :] // ATT_HEAD_DIM).astype(BF16)
    qn = jnp.concatenate([q_norm, q_norm]).reshape(1, LANES)
    kn = jnp.concatenate([k_norm, jnp.ones_like(k_norm)]).reshape(1, LANES)
    const = lambda bi, i: (0, 0)
    tok = lambda bi, i: (bi, i, 0)
    head_tok = lambda bi, i: (bi, 0, i, 0)
    return pl.pallas_call(
        _prep_kernel,
        grid=(b, s // tm),
        in_specs=[
            pl.BlockSpec((None, tm, 8 * LANES), tok),
            pl.BlockSpec((tm, LANES), lambda bi, i: (i, 0)),
            pl.BlockSpec((tm, LANES), lambda bi, i: (i, 0)),
            pl.BlockSpec((LANES, LANES), const),
            pl.BlockSpec((1, LANES), const),
            pl.BlockSpec((1, LANES), const),
        ],
        out_specs=[
            pl.BlockSpec((None, ATT_HEADS, tm, ATT_HEAD_DIM), head_tok),
            pl.BlockSpec((None, IDX_HEADS, tm, IDX_DIM), head_tok),
            pl.BlockSpec((None, tm, ATT_HEAD_DIM), tok),
            pl.BlockSpec((None, tm, LANES), tok),
            pl.BlockSpec((None, tm, IDX_DIM), tok),
        ],
        out_shape=[
            jax.ShapeDtypeStruct((b, ATT_HEADS, s, ATT_HEAD_DIM), BF16),
            jax.ShapeDtypeStruct((b, IDX_HEADS, s, IDX_DIM), BF16),
            jax.ShapeDtypeStruct((b, s, ATT_HEAD_DIM), BF16),
            jax.ShapeDtypeStruct((b, s, LANES), BF16),
            jax.ShapeDtypeStruct((b, s, IDX_DIM), BF16),
        ],
        compiler_params=_params("parallel", "parallel"),
        name="dsa_prep",
    )(proj, cosf, sinf, gmat, qn, kn)


def _sortable_key(score):
    score = jnp.where(score == 0.0, 0.0, score)
    bits = lax.bitcast_convert_type(score, jnp.int32)
    return bits ^ ((bits >> 31) & jnp.int32(0x7FFFFFFF))


def _dsa_kernel(q_ref, qi_ref, w_ref, ki_ref, k_ref, v_ref, o_ref,
                hi_scr, lo_scr, thr_scr, top_scr, need_scr, m_scr, acc_scr, *, qblk, topk, rows):
    qb = pl.program_id(1)
    nch = qb + 1
    nrep = qblk // LANES
    row = lax.broadcasted_iota(jnp.int32, (qblk, qblk), 0)
    col = lax.broadcasted_iota(jnp.int32, (qblk, qblk), 1)
    causal = col <= row

    wblk = w_ref[...]
    wcols = [jnp.broadcast_to(wblk[:, W_LANE + h:W_LANE + h + 1] * (IDX_HEADS ** -0.5), (qblk, qblk))
             for h in range(IDX_HEADS)]

    def score_chunk(j, diag):
        ks = pl.multiple_of(j * qblk, qblk)
        kc = ki_ref[pl.ds(ks, qblk), :]
        sc = jnp.zeros((qblk, qblk), F32)
        for h in range(IDX_HEADS):
            logit = lax.dot_general(qi_ref[h], kc, _NT, preferred_element_type=F32)
            sc = sc + jnp.maximum(logit, 0.0) * wcols[h]
        if diag:
            sc = jnp.where(causal, sc, -jnp.inf)
        key = _sortable_key(sc)
        hi_scr[:, pl.ds(ks, qblk)] = (key >> 16).astype(jnp.int16)
        lo_scr[:, pl.ds(ks, qblk)] = ((key & 0xFFFF) + I16_MIN).astype(jnp.int16)

    def score_body(j, carry):
        score_chunk(j, False)
        return carry

    lax.fori_loop(0, qb, score_body, 0)
    score_chunk(qb, True)

    one16 = jnp.ones((), BF16)
    zero16 = jnp.zeros((), BF16)
    lane_ones = jnp.ones((LANES, LANES), BF16)

    groups = list(range(0, qblk, rows))

    def hits16(src_scr, r0, cand):
        def slab(acc, ks, c):
            blk = src_scr[r0:r0 + rows, pl.ds(ks + c * LANES, LANES)]
            return acc + jnp.where(blk >= cand, one16, zero16)

        def pair_body(j, acc):
            ks = pl.multiple_of(j * (2 * qblk), 2 * qblk)
            for c in range(2 * nrep):
                acc = slab(acc, ks, c)
            return acc

        def single_body(j, acc):
            ks = pl.multiple_of(j * qblk, qblk)
            for c in range(nrep):
                acc = slab(acc, ks, c)
            return acc

        npair = nch // 2
        acc = lax.fori_loop(0, npair, pair_body, jnp.zeros((rows, LANES), BF16))
        return lax.fori_loop(2 * npair, nch, single_body, acc)

    def counts16(src_scr, cands):
        accs = [hits16(src_scr, r0, cand) for r0, cand in zip(groups, cands)]
        return [jnp.dot(acc, lane_ones, preferred_element_type=F32) for acc in accs]

    def bisect16(count_fn, targets):
        def body(bi, carry):
            thrs, aboves = carry
            step = lax.shift_left(jnp.int32(1), 15 - bi)
            cands = [thr + step for thr in thrs]
            cnts = count_fn([cand.astype(jnp.int16) for cand in cands])
            takes = [cnt >= target for cnt, target in zip(cnts, targets)]
            return (tuple(jnp.where(take, cand, thr) for take, cand, thr in zip(takes, cands, thrs)),
                    tuple(jnp.where(take, above, cnt) for take, above, cnt in zip(takes, aboves, cnts)))
        start = (tuple(jnp.full((rows, LANES), I16_MIN, jnp.int32) for _ in groups),
                 tuple(jnp.zeros((rows, LANES), F32) for _ in groups))
        thrs, aboves = lax.fori_loop(0, 16, body, start)
        return [thr.astype(jnp.int16) for thr in thrs], aboves

    thr_hi16, above_hi = bisect16(lambda cands: counts16(hi_scr, cands), [topk] * len(groups))
    quotas = [topk - cnt for cnt in above_hi]
    for r0, hi16 in zip(groups, thr_hi16):
        thr_scr[0, r0:r0 + rows, :] = hi16

    min16 = jnp.full((), I16_MIN, jnp.int16)
    sub = 16 * SUBLANES

    def collect(rr):
        thr_sub = thr_scr[0, rr:rr + sub, :]

        def body(j, carry):
            first, second, third, lost = carry
            ks = pl.multiple_of(j * qblk, qblk)
            for c in range(nrep):
                sl = (slice(rr, rr + sub), pl.ds(ks + c * LANES, LANES))
                x = jnp.where(hi_scr[sl] == thr_sub, lo_scr[sl], min16)
                gt1, gt2, gt3 = x > first, x > second, x > third
                out = jnp.where(gt3, third, x)
                third = jnp.where(gt2, second, jnp.where(gt3, x, third))
                second = jnp.where(gt1, first, jnp.where(gt2, x, second))
                first = jnp.where(gt1, x, first)
                lost = jnp.where(out > lost, out, lost)
            return first, second, third, lost

        lowest = jnp.full((sub, LANES), I16_MIN, jnp.int16)
        first, second, third, lost = lax.fori_loop(0, nch, body, (lowest,) * 4)
        top_scr[0, rr:rr + sub, :] = first
        top_scr[1, rr:rr + sub, :] = second
        top_scr[2, rr:rr + sub, :] = third
        return jnp.max(lost.astype(F32))

    lost_max = collect(0)
    for rr in range(sub, qblk, sub):
        lost_max = jnp.maximum(lost_max, collect(rr))
    lossless = lost_max == float(I16_MIN)

    def counts_top(cands):
        outs = []
        for r0, cand in zip(groups, cands):
            acc = jnp.zeros((rows, LANES), BF16)
            for t in range(3):
                acc = acc + jnp.where(top_scr[t, r0:r0 + rows, :] >= cand, one16, zero16)
            outs.append(jnp.dot(acc, lane_ones, preferred_element_type=F32))
        return outs

    def finish(thr_lo16, above_lo):
        for r0, lo16, quota, above in zip(groups, thr_lo16, quotas, above_lo):
            thr_scr[1, r0:r0 + rows, :] = lo16
            need_scr[r0:r0 + rows, :] = quota - above

    @pl.when(lossless)
    def _():
        finish(*bisect16(counts_top, quotas))

    @pl.when(jnp.logical_not(lossless))
    def _():
        def pin_body(j, carry):
            ks = pl.multiple_of(j * qblk, qblk)
            for r0, thr16 in zip(groups, thr_hi16):
                for c in range(nrep):
                    sl = (slice(r0, r0 + rows), pl.ds(ks + c * LANES, LANES))
                    lo_scr[sl] = jnp.where(hi_scr[sl] == thr16, lo_scr[sl], min16)
            return carry

        lax.fori_loop(0, nch, pin_body, 0)
        finish(*bisect16(lambda cands: counts16(lo_scr, cands), quotas))

    m_scr[...] = jnp.full(m_scr.shape, NEG_BIG, F32)
    acc_scr[...] = jnp.zeros(acc_scr.shape, F32)
    thr_hi_all = thr_scr[0]
    thr_lo_all = thr_scr[1]
    need_all = need_scr[...]
    nsub = MXU_DIM // LANES
    prefix_ones = jnp.where(lax.broadcasted_iota(jnp.int32, (MXU_DIM, MXU_DIM), 0)
                            <= lax.broadcasted_iota(jnp.int32, (MXU_DIM, MXU_DIM), 1), 1.0, 0.0).astype(BF16)
    ninf16 = jnp.full((), -jnp.inf, BF16)

    def attn_chunk(j, seen, diag):
        ks = pl.multiple_of(j * qblk, qblk)
        above, equal = [], []
        for c in range(nrep):
            hi_c = hi_scr[:, pl.ds(ks + c * LANES, LANES)]
            lo_c = lo_scr[:, pl.ds(ks + c * LANES, LANES)]
            bucket = hi_c == thr_hi_all
            above.append(jnp.where(hi_c > thr_hi_all, zero16,
                                   jnp.where(bucket, jnp.where(lo_c > thr_lo_all, zero16, ninf16), ninf16)))
            equal.append(jnp.where(bucket, jnp.where(lo_c == thr_lo_all, one16, zero16), zero16))
        parts = []
        for u in range(qblk // MXU_DIM):
            eq_u = jnp.concatenate(equal[u * nsub:(u + 1) * nsub], axis=1)
            rank = jnp.dot(eq_u, prefix_ones, preferred_element_type=F32)
            room = jnp.concatenate([need_all - seen] * nsub, axis=1)
            fits = jnp.where(rank <= room, 0.0, -jnp.inf)
            above_u = jnp.concatenate(above[u * nsub:(u + 1) * nsub], axis=1).astype(F32)
            parts.append(jnp.where(eq_u.astype(F32) > 0.5, fits, above_u))
            seen = seen + jnp.max(rank, axis=-1, keepdims=True)
        bias = jnp.concatenate(parts, axis=1)
        if diag:
            bias = jnp.where(causal, bias, -jnp.inf)
        kc = k_ref[pl.ds(ks, qblk), :]
        vc = v_ref[pl.ds(ks, qblk), :]
        for h in range(ATT_HEADS):
            s = lax.dot_general(q_ref[h], kc, _NT, preferred_element_type=F32) + bias
            m_old = m_scr[h]
            m_new = jnp.maximum(m_old, jnp.max(s, axis=-1, keepdims=True))
            alpha = jnp.exp2(m_old - m_new)
            p = jnp.exp2(s - jnp.concatenate([m_new] * nrep, axis=1))
            acc_scr[h] = alpha * acc_scr[h] + jnp.dot(p.astype(BF16), vc, preferred_element_type=F32)
            m_scr[h] = m_new
        return seen

    seen = lax.fori_loop(0, qb, lambda j, seen: attn_chunk(j, seen, False), jnp.zeros((qblk, LANES), F32))
    attn_chunk(qb, seen, True)

    outs = []
    for h in range(ATT_HEADS):
        acc = acc_scr[h]
        outs.append(acc[:, :ATT_HEAD_DIM] / acc[:, ATT_HEAD_DIM:])
    o_ref[...] = jnp.concatenate(outs, axis=1).astype(o_ref.dtype)


def _dsa(q, qi, proj, ki, k, v, *, qblk=512, rows=256):
    b, _, s, _ = q.shape
    topk = min(TOPK_MAX, s // 4)
    assert qblk >= topk and s % qblk == 0 and qblk % rows == 0 and qblk % MXU_DIM == 0
    assert s // LANES <= 256
    kern = functools.partial(_dsa_kernel, qblk=qblk, topk=float(topk), rows=rows)
    head_tok = lambda bi, i: (bi, 0, i, 0)
    full = lambda bi, i: (bi, 0, 0)
    once = pl.Buffered(1)
    return pl.pallas_call(
        kern,
        grid=(b, s // qblk),
        in_specs=[
            pl.BlockSpec((None, ATT_HEADS, qblk, ATT_HEAD_DIM), head_tok),
            pl.BlockSpec((None, IDX_HEADS, qblk, IDX_DIM), head_tok),
            pl.BlockSpec((None, qblk, LANES), lambda bi, i: (bi, i, GATE_BLOCK)),
            pl.BlockSpec((None, s, IDX_DIM), full, pipeline_mode=once),
            pl.BlockSpec((None, s, ATT_HEAD_DIM), full, pipeline_mode=once),
            pl.BlockSpec((None, s, LANES), full, pipeline_mode=once),
        ],
        out_specs=pl.BlockSpec((None, qblk, ATT_WIDTH), lambda bi, i: (bi, i, 0)),
        out_shape=jax.ShapeDtypeStruct((b, s, ATT_WIDTH), BF16),
        scratch_shapes=[
            pltpu.VMEM((qblk, s + LANES), jnp.int16),
            pltpu.VMEM((qblk, s + LANES), jnp.int16),
            pltpu.VMEM((2, qblk, LANES), jnp.int16),
            pltpu.VMEM((3, qblk, LANES), jnp.int16),
            pltpu.VMEM((qblk, LANES), F32),
            pltpu.VMEM((ATT_HEADS, qblk, LANES), F32),
            pltpu.VMEM((ATT_HEADS, qblk, LANES), F32),
        ],
        compiler_params=_params("parallel", "arbitrary"),
        name="dsa_attention",
    )(q, qi, proj, ki, k, v)


def _log_sigmoid(x):
    return jnp.minimum(x, 0.0) - jnp.log(1.0 + jnp.exp(-jnp.abs(x)))


def _mlstm_kernel(qk_ref, v_ref, og_ref, g_ref, cw_ref, cb_ref, gb_ref, mn_ref, o_ref,
                  xe_scr, c_scr, n_scr, m_scr):
    L = CHUNK
    d = M_HEAD_DIM
    hi = lax.Precision.HIGHEST

    @pl.when(pl.program_id(1) == 0)
    def _():
        xe_scr[0:SUBLANES, :] = jnp.zeros((SUBLANES, 2 * M_WIDTH), F32)
        c_scr[...] = jnp.zeros(c_scr.shape, F32)
        n_scr[...] = jnp.zeros(n_scr.shape, F32)
        m_scr[...] = jnp.zeros(m_scr.shape, F32)

    xe_scr[SUBLANES:SUBLANES + L, :] = qk_ref[...]
    y = jnp.broadcast_to(cb_ref[...], (L, 2 * M_WIDTH))
    for j in range(CONV_K):
        off = SUBLANES - (CONV_K - 1) + j
        y = y + cw_ref[j:j + 1, :] * xe_scr[off:off + L, :]
    xe_scr[0:SUBLANES, :] = xe_scr[L:L + SUBLANES, :]
    qk = y * _sigmoid(y)

    g = g_ref[...] + gb_ref[...]
    ls = _log_sigmoid(g)
    row = lax.broadcasted_iota(jnp.int32, (L, L), 0)
    col = lax.broadcasted_iota(jnp.int32, (L, L), 1)
    lower = row >= col
    tril = jnp.where(lower, 1.0, 0.0)
    triu = jnp.where(row <= col, 1.0, 0.0)
    b_cols = jnp.dot(tril, ls, preferred_element_type=F32, precision=hi)
    g_t = g.T
    b_rows = jnp.dot(ls.T, triu, preferred_element_type=F32, precision=hi)

    for h in range(M_HEADS):
        sl = slice(h * d, (h + 1) * d)
        qh = qk[:, sl]
        kh = qk[:, M_WIDTH + h * d:M_WIDTH + (h + 1) * d] * (d ** -0.5)
        vh = v_ref[:, sl]
        qb16 = qh.astype(BF16)
        kb16 = kh.astype(BF16)
        vb16 = vh.astype(BF16)
        b_col = b_cols[:, F_LANE + h:F_LANE + h + 1]
        i_col = g[:, I_LANE + h:I_LANE + h + 1]
        b_row = b_rows[F_LANE + h:F_LANE + h + 1, :]
        i_row = g_t[I_LANE + h:I_LANE + h + 1, :]
        b_last = b_row[:, L - 1:L]

        c_prev = c_scr[h]
        n_prev = n_scr[h]
        m_prev = m_scr[h]

        a_row = b_last - b_row + i_row
        a_max = jnp.max(a_row, axis=-1, keepdims=True)
        wa_col = jnp.exp(b_last - b_col + i_col - a_max)
        kw = kh * wa_col
        c_chunk = lax.dot_general(kw.astype(BF16), vb16, _TN, preferred_element_type=F32)
        n_chunk = jnp.sum(kw, axis=0, keepdims=True)

        g_col = b_col + m_prev
        dmat = jnp.where(lower, b_col + (i_row - b_row), -jnp.inf)
        m_t = jnp.maximum(g_col, jnp.max(dmat, axis=-1, keepdims=True))
        w_intra = jnp.exp(dmat - m_t)
        w_inter = jnp.exp(g_col - m_t)
        s = lax.dot_general(qb16, kb16, _NT, preferred_element_type=F32) * w_intra
        num = w_inter * jnp.dot(qb16, c_prev.astype(BF16), preferred_element_type=F32) \
            + jnp.dot(s.astype(BF16), vb16, preferred_element_type=F32)
        den = w_inter * jnp.sum(qh * n_prev, axis=-1, keepdims=True) + jnp.sum(s, axis=-1, keepdims=True)
        hh = num / jnp.maximum(jnp.abs(den), jnp.exp(-m_t))
        hn = _rms_rows(hh, mn_ref[:, sl])
        o_ref[:, sl] = (hn * _sigmoid(og_ref[:, sl])).astype(o_ref.dtype)

        m_new = jnp.maximum(b_last + m_prev, a_max)
        s_old = jnp.exp(b_last + m_prev - m_new)
        s_new = jnp.exp(a_max - m_new)
        c_scr[h] = s_old * c_prev + s_new * c_chunk
        n_scr[h] = s_old * n_prev + s_new * n_chunk
        m_scr[h] = m_new


def _mlstm(proj, conv_w, conv_b, gate_b, m_norm):
    b, s, _ = proj.shape
    L = CHUNK
    cw = jnp.zeros((SUBLANES, 2 * M_WIDTH), F32).at[:CONV_K].set(conv_w.reshape(CONV_K, 2 * M_WIDTH))
    gb = jnp.zeros((1, LANES), F32).at[0, I_LANE:I_LANE + 2 * M_HEADS].set(gate_b)
    const = lambda bi, c: (0, 0)
    return pl.pallas_call(
        _mlstm_kernel,
        grid=(b, s // L),
        in_specs=[
            pl.BlockSpec((None, L, 2 * M_WIDTH), lambda bi, c: (bi, c, 1)),
            pl.BlockSpec((None, L, M_WIDTH), lambda bi, c: (bi, c, 4)),
            pl.BlockSpec((None, L, M_WIDTH), lambda bi, c: (bi, c, 5)),
            pl.BlockSpec((None, L, LANES), lambda bi, c: (bi, c, GATE_BLOCK)),
            pl.BlockSpec((SUBLANES, 2 * M_WIDTH), const),
            pl.BlockSpec((1, 2 * M_WIDTH), const),
            pl.BlockSpec((1, LANES), const),
            pl.BlockSpec((1, M_WIDTH), const),
        ],
        out_specs=pl.BlockSpec((None, L, M_WIDTH), lambda bi, c: (bi, c, 0)),
        out_shape=jax.ShapeDtypeStruct((b, s, M_WIDTH), BF16),
        scratch_shapes=[
            pltpu.VMEM((L + SUBLANES, 2 * M_WIDTH), F32),
            pltpu.VMEM((M_HEADS, M_HEAD_DIM, M_HEAD_DIM), F32),
            pltpu.VMEM((M_HEADS, 1, M_HEAD_DIM), F32),
            pltpu.VMEM((M_HEADS, 1, 1), F32),
        ],
        compiler_params=_params("parallel", "arbitrary"),
        name="mlstm",
    )(proj, proj, proj, proj, cw, conv_b.reshape(1, -1), gb, m_norm.reshape(1, M_WIDTH))


def _outproj_kernel(att_ref, hm_ref, x_ref, w_ref, o_ref):
    cat = jnp.concatenate([att_ref[...], hm_ref[...]], axis=1)
    o_ref[...] = x_ref[...] + jnp.dot(cat, w_ref[...], preferred_element_type=F32)


def _outproj(att2, hm2, x2, w_out, *, tm=512):
    t, d = x2.shape
    mix = att2.shape[1] + hm2.shape[1]
    return pl.pallas_call(
        _outproj_kernel,
        grid=(t // tm,),
        in_specs=[
            pl.BlockSpec((tm, att2.shape[1]), lambda i: (i, 0)),
            pl.BlockSpec((tm, hm2.shape[1]), lambda i: (i, 0)),
            pl.BlockSpec((tm, d), lambda i: (i, 0)),
            pl.BlockSpec((mix, d), lambda i: (0, 0), pipeline_mode=pl.Buffered(1)),
        ],
        out_specs=pl.BlockSpec((tm, d), lambda i: (i, 0)),
        out_shape=jax.ShapeDtypeStruct((t, d), F32),
        compiler_params=_params("parallel"),
        name="outproj",
    )(att2, hm2, x2, w_out.astype(BF16))


def kernel(x, ffn1_norm, ffn1_w_gate, ffn1_w_up, ffn1_w_down, mix_norm, w_in, conv_w, conv_b, gate_b,
           q_norm, k_norm, m_norm, w_out, ffn2_norm, ffn2_w_gate, ffn2_w_up, ffn2_w_down):
    b, s, d = x.shape
    depth = w_in.shape[0]
    x2 = x.reshape(b * s, d)
    for l in range(depth):
        x2 = _ffn(x2, ffn1_norm[l], ffn1_w_gate[l], ffn1_w_up[l], ffn1_w_down[l])
        proj = _inproj(x2, mix_norm[l], _arrange_w_in(w_in[l])).reshape(b, s, PROJ_COLS)
        q, qi, k, v, ki = _prep(proj, q_norm[l], k_norm[l])
        att = _dsa(q, qi, proj, ki, k, v)
        hm = _mlstm(proj, conv_w[l], conv_b[l], gate_b[l], m_norm[l])
        x2 = _outproj(att.reshape(b * s, ATT_WIDTH), hm.reshape(b * s, M_WIDTH), x2, w_out[l])
        x2 = _ffn(x2, ffn2_norm[l], ffn2_w_gate[l], ffn2_w_up[l], ffn2_w_down[l])
    return x2.reshape(b, s, d)
```

```python
import functools

import jax
import jax.numpy as jnp
from jax import lax
from jax.experimental import pallas as pl
from jax.experimental.pallas import tpu as pltpu

ATT_HEADS = 8
ATT_HEAD_DIM = 64
ATT_WIDTH = ATT_HEADS * ATT_HEAD_DIM
IDX_HEADS = 4
IDX_DIM = 64
TOPK_MAX = 256
M_HEADS = 4
M_HEAD_DIM = 128
M_WIDTH = M_HEADS * M_HEAD_DIM
CHUNK = 128
CONV_K = 4
ROPE_THETA = 10000.0
EPS = 1e-6

LANES = 128
SUBLANES = 8
MXU_DIM = 256
VMEM_LIMIT_BYTES = 56 * 1024 * 1024

PROJ_COLS = 3072
GATE_BLOCK = 7
W_LANE = 64
I_LANE = 68
F_LANE = 72

I16_MIN = -(2 ** 15)
NEG_BIG = -1e30
LOG2E = 1.4426950408889634

BF16 = jnp.bfloat16
F32 = jnp.float32

_NT = (((1,), (1,)), ((), ()))
_TN = (((0,), (0,)), ((), ()))


def _params(*sem):
    return pltpu.CompilerParams(dimension_semantics=sem, vmem_limit_bytes=VMEM_LIMIT_BYTES)


def _sigmoid(x):
    return 1.0 / (1.0 + jnp.exp(-x))


def _rms_rows(x, gain):
    return x * lax.rsqrt(jnp.mean(x * x, axis=-1, keepdims=True) + EPS) * gain


def _ffn_kernel(x_ref, g_ref, wg_ref, wu_ref, wd_ref, o_ref, *, chunks):
    x = x_ref[...]
    h = _rms_rows(x, g_ref[...]).astype(BF16)
    acc = jnp.zeros(x.shape, F32)
    f0 = 0
    for fc in chunks:
        g = jnp.dot(h, wg_ref[:, f0:f0 + fc], preferred_element_type=F32)
        u = jnp.dot(h, wu_ref[:, f0:f0 + fc], preferred_element_type=F32)
        a = (g * _sigmoid(g) * u).astype(BF16)
        acc = acc + jnp.dot(a, wd_ref[f0:f0 + fc, :], preferred_element_type=F32)
        f0 += fc
    o_ref[...] = x + 0.5 * acc


def _ffn_chunks(d_ff):
    step = 512
    chunks = [step] * (d_ff // step)
    if d_ff % step:
        chunks.append(d_ff % step)
    return tuple(chunks)


def _ffn(x2, gain, w_gate, w_up, w_down, *, tm=512):
    t, d = x2.shape
    d_ff = w_gate.shape[1]
    const = lambda i: (0, 0)
    return pl.pallas_call(
        functools.partial(_ffn_kernel, chunks=_ffn_chunks(d_ff)),
        grid=(t // tm,),
        in_specs=[
            pl.BlockSpec((tm, d), lambda i: (i, 0)),
            pl.BlockSpec((1, d), const),
            pl.BlockSpec((d, d_ff), const, pipeline_mode=pl.Buffered(1)),
            pl.BlockSpec((d, d_ff), const, pipeline_mode=pl.Buffered(1)),
            pl.BlockSpec((d_ff, d), const, pipeline_mode=pl.Buffered(1)),
        ],
        out_specs=pl.BlockSpec((tm, d), lambda i: (i, 0)),
        out_shape=jax.ShapeDtypeStruct((t, d), F32),
        compiler_params=_params("parallel"),
        name="ffn",
    )(x2, gain.reshape(1, d), w_gate.astype(BF16), w_up.astype(BF16), w_down.astype(BF16))


def _inproj_kernel(x_ref, g_ref, w_ref, o_ref):
    h = _rms_rows(x_ref[...], g_ref[...]).astype(BF16)
    o_ref[...] = jnp.dot(h, w_ref[...], preferred_element_type=F32)


def _inproj(x2, gain, w_arranged, *, tm=512):
    t, d = x2.shape
    const = lambda i: (0, 0)
    return pl.pallas_call(
        _inproj_kernel,
        grid=(t // tm,),
        in_specs=[
            pl.BlockSpec((tm, d), lambda i: (i, 0)),
            pl.BlockSpec((1, d), const),
            pl.BlockSpec((d, PROJ_COLS), const, pipeline_mode=pl.Buffered(1)),
        ],
        out_specs=pl.BlockSpec((tm, PROJ_COLS), lambda i: (i, 0)),
        out_shape=jax.ShapeDtypeStruct((t, PROJ_COLS), F32),
        compiler_params=_params("parallel"),
        name="inproj",
    )(x2, gain.reshape(1, d), w_arranged)


def _arrange_w_in(w_in):
    d = w_in.shape[0]
    head = ATT_WIDTH + 2 * ATT_HEAD_DIM + IDX_HEADS * IDX_DIM + IDX_DIM + IDX_HEADS
    body = 4 * M_WIDTH
    gates = w_in[:, head + body:head + body + 2 * M_HEADS]
    pad = jnp.zeros((d, PROJ_COLS - head - body - 2 * M_HEADS), w_in.dtype)
    return jnp.concatenate([w_in[:, :head], gates, pad, w_in[:, head:head + body]], axis=1).astype(BF16)


def _rope(x, cosf, sinf):
    lane = lax.broadcasted_iota(jnp.int32, x.shape, 1)
    lower = (lane % ATT_HEAD_DIM) < (ATT_HEAD_DIM // 2)
    width = x.shape[1]
    partner = jnp.where(lower, pltpu.roll(x, width - ATT_HEAD_DIM // 2, 1), pltpu.roll(x, ATT_HEAD_DIM // 2, 1))
    return x * cosf + partner * sinf


def _group_mean_sq(x, gmat):
    sq = x * x
    hi = sq.astype(BF16)
    lo = (sq - hi.astype(F32)).astype(BF16)
    tot = jnp.dot(hi, gmat, preferred_element_type=F32) + jnp.dot(lo, gmat, preferred_element_type=F32)
    return tot * (1.0 / ATT_HEAD_DIM)


def _prep_kernel(p_ref, cos_ref, sin_ref, gmat_ref, qn_ref, kn_ref,
                 q_ref, qi_ref, k_ref, v_ref, ki_ref):
    cosf = cos_ref[...]
    sinf = sin_ref[...]
    gmat = gmat_ref[...]
    qk_scale = ATT_HEAD_DIM ** -0.5 * LOG2E
    idx_scale = IDX_DIM ** -0.5
    for s in range(ATT_WIDTH // LANES):
        x = p_ref[:, s * LANES:(s + 1) * LANES]
        xn = x * lax.rsqrt(_group_mean_sq(x, gmat) + EPS) * qn_ref[...]
        xr = (_rope(xn, cosf, sinf) * qk_scale).astype(BF16)
        q_ref[2 * s] = xr[:, :ATT_HEAD_DIM]
        q_ref[2 * s + 1] = xr[:, ATT_HEAD_DIM:]
    kv = p_ref[:, ATT_WIDTH:ATT_WIDTH + LANES]
    kn = kv * lax.rsqrt(_group_mean_sq(kv, gmat) + EPS) * kn_ref[...]
    kr = _rope(kn, cosf, sinf)
    k_ref[...] = kr[:, :ATT_HEAD_DIM].astype(BF16)
    lane = lax.broadcasted_iota(jnp.int32, kv.shape, 1)
    v_ref[...] = jnp.where(lane < ATT_HEAD_DIM, pltpu.roll(kv, ATT_HEAD_DIM, 1), 1.0).astype(BF16)
    base = ATT_WIDTH + LANES
    for s in range(IDX_HEADS * IDX_DIM // LANES):
        x = p_ref[:, base + s * LANES:base + (s + 1) * LANES]
        xr = (_rope(x, cosf, sinf) * idx_scale).astype(BF16)
        qi_ref[2 * s] = xr[:, :IDX_DIM]
        qi_ref[2 * s + 1] = xr[:, IDX_DIM:]
    gk = p_ref[:, GATE_BLOCK * LANES:(GATE_BLOCK + 1) * LANES]
    ki_ref[...] = _rope(gk, cosf, sinf)[:, :IDX_DIM].astype(BF16)


def _rope_tables(seq):
    half = ATT_HEAD_DIM // 2
    inv_freq = ROPE_THETA ** (-jnp.arange(half, dtype=F32) / half)
    ang = jnp.arange(seq, dtype=jnp.int32).astype(F32)[:, None] * inv_freq[None, :]
    cos = jnp.cos(ang)
    sin = jnp.sin(ang)
    cosf = jnp.concatenate([cos, cos, cos, cos], axis=1)
    sinf = jnp.concatenate([-sin, sin, -sin, sin], axis=1)
    return cosf, sinf


def _prep(proj, q_norm, k_norm, *, tm=512):
    b, s, _ = proj.shape
    cosf, sinf = _rope_tables(s)
    lane = jnp.arange(LANES)
    gmat = (lane[:, None] // ATT_HEAD_DIM == lane[None, :] // ATT_HEAD_DIM).astype(BF16)
    qn = jnp.concatenate([q_norm, q_norm]).reshape(1, LANES)
    kn = jnp.concatenate([k_norm, jnp.ones_like(k_norm)]).reshape(1, LANES)
    const = lambda bi, i: (0, 0)
    tok = lambda bi, i: (bi, i, 0)
    head_tok = lambda bi, i: (bi, 0, i, 0)
    return pl.pallas_call(
        _prep_kernel,
        grid=(b, s // tm),
        in_specs=[
            pl.BlockSpec((None, tm, 8 * LANES), tok),
            pl.BlockSpec((tm, LANES), lambda bi, i: (i, 0)),
            pl.BlockSpec((tm, LANES), lambda bi, i: (i, 0)),
            pl.BlockSpec((LANES, LANES), const),
            pl.BlockSpec((1, LANES), const),
            pl.BlockSpec((1, LANES), const),
        ],
        out_specs=[
            pl.BlockSpec((None, ATT_HEADS, tm, ATT_HEAD_DIM), head_tok),
            pl.BlockSpec((None, IDX_HEADS, tm, IDX_DIM), head_tok),
            pl.BlockSpec((None, tm, ATT_HEAD_DIM), tok),
            pl.BlockSpec((None, tm, LANES), tok),
            pl.BlockSpec((None, tm, IDX_DIM), tok),
        ],
        out_shape=[
            jax.ShapeDtypeStruct((b, ATT_HEADS, s, ATT_HEAD_DIM), BF16),
            jax.ShapeDtypeStruct((b, IDX_HEADS, s, IDX_DIM), BF16),
            jax.ShapeDtypeStruct((b, s, ATT_HEAD_DIM), BF16),
            jax.ShapeDtypeStruct((b, s, LANES), BF16),
            jax.ShapeDtypeStruct((b, s, IDX_DIM), BF16),
        ],
        compiler_params=_params("parallel", "parallel"),
        name="dsa_prep",
    )(proj, cosf, sinf, gmat, qn, kn)


def _sortable_key(score):
    bits = lax.bitcast_convert_type(score, jnp.int32)
    sign = bits >> 31
    return (bits ^ (sign & jnp.int32(0x7FFFFFFF))) - sign


def _dsa_kernel(q_ref, qi_ref, w_ref, ki_ref, k_ref, v_ref, o_ref,
                hi_scr, lo_scr, thr_scr, top_scr, need_scr, m_scr, acc_scr, *, qblk, topk, rows):
    qb = pl.program_id(1)
    nch = qb + 1
    nrep = qblk // LANES
    row = lax.broadcasted_iota(jnp.int32, (qblk, qblk), 0)
    col = lax.broadcasted_iota(jnp.int32, (qblk, qblk), 1)
    causal = col <= row

    wblk = w_ref[...]
    wcols = [jnp.broadcast_to(wblk[:, W_LANE + h:W_LANE + h + 1] * (IDX_HEADS ** -0.5), (qblk, qblk))
             for h in range(IDX_HEADS)]

    def score_chunk(j, diag):
        ks = pl.multiple_of(j * qblk, qblk)
        kc = ki_ref[pl.ds(ks, qblk), :]
        sc = jnp.zeros((qblk, qblk), F32)
        for h in range(IDX_HEADS):
            logit = lax.dot_general(qi_ref[h], kc, _NT, preferred_element_type=F32)
            sc = sc + jnp.maximum(logit, 0.0) * wcols[h]
        if diag:
            sc = jnp.where(causal, sc, -jnp.inf)
        key = _sortable_key(sc)
        hi_scr[:, pl.ds(ks, qblk)] = (key >> 16).astype(jnp.int16)
        lo_scr[:, pl.ds(ks, qblk)] = ((key & 0xFFFF) + I16_MIN).astype(jnp.int16)

    def score_body(j, carry):
        score_chunk(j, False)
        return carry

    lax.fori_loop(0, qb, score_body, 0)
    score_chunk(qb, True)

    one16 = jnp.ones((), BF16)
    zero16 = jnp.zeros((), BF16)
    lane_ones = jnp.ones((LANES, LANES), BF16)

    groups = list(range(0, qblk, rows))

    def hits16(src_scr, r0, cand):
        def slab(acc, ks, c):
            blk = src_scr[r0:r0 + rows, pl.ds(ks + c * LANES, LANES)]
            return acc + jnp.where(blk >= cand, one16, zero16)

        def pair_body(j, acc):
            ks = pl.multiple_of(j * (2 * qblk), 2 * qblk)
            for c in range(2 * nrep):
                acc = slab(acc, ks, c)
            return acc

        def single_body(j, acc):
            ks = pl.multiple_of(j * qblk, qblk)
            for c in range(nrep):
                acc = slab(acc, ks, c)
            return acc

        npair = nch // 2
        acc = lax.fori_loop(0, npair, pair_body, jnp.zeros((rows, LANES), BF16))
        return lax.fori_loop(2 * npair, nch, single_body, acc)

    def counts16(src_scr, cands):
        accs = [hits16(src_scr, r0, cand) for r0, cand in zip(groups, cands)]
        return [jnp.dot(acc, lane_ones, preferred_element_type=F32) for acc in accs]

    def bisect16(count_fn, targets):
        targets16 = [jnp.broadcast_to(jnp.asarray(target, F32), (rows, LANES)).astype(BF16) for target in targets]

        def body(bi, carry):
            thrs, aboves = carry
            bit = jnp.where(bi == 0, jnp.int32(I16_MIN), lax.shift_left(jnp.int32(1), 15 - bi))
            step = jnp.full((rows, LANES), bit, jnp.int32).astype(jnp.int16)
            cands = [thr ^ step for thr in thrs]
            cnts = [cnt.astype(BF16) for cnt in count_fn(cands)]
            takes = [cnt >= target for cnt, target in zip(cnts, targets16)]
            return (tuple(jnp.where(take, cand, thr) for take, cand, thr in zip(takes, cands, thrs)),
                    tuple(jnp.where(take, above, cnt) for take, above, cnt in zip(takes, aboves, cnts)))
        start = (tuple(jnp.full((rows, LANES), I16_MIN, jnp.int16) for _ in groups),
                 tuple(jnp.zeros((rows, LANES), BF16) for _ in groups))
        thrs, aboves = lax.fori_loop(0, 16, body, start)
        return list(thrs), [above.astype(F32) for above in aboves]

    min16 = jnp.full((), I16_MIN, jnp.int16)
    sub = 16 * SUBLANES
    thr_hi16, above_hi = bisect16(lambda cands: counts16(hi_scr, cands), [topk] * len(groups))
    quotas = [topk - cnt for cnt in above_hi]
    for r0, hi16 in zip(groups, thr_hi16):
        thr_scr[0, r0:r0 + rows, :] = hi16

    def collect(rr):
        thr_sub = thr_scr[0, rr:rr + sub, :]

        def body(j, carry):
            first, second, third, lost = carry
            ks = pl.multiple_of(j * qblk, qblk)
            for c in range(nrep):
                sl = (slice(rr, rr + sub), pl.ds(ks + c * LANES, LANES))
                x = jnp.where(hi_scr[sl] == thr_sub, lo_scr[sl], min16)
                gt1, gt2, gt3 = x > first, x > second, x > third
                out = jnp.where(gt3, third, x)
                third = jnp.where(gt2, second, jnp.where(gt3, x, third))
                second = jnp.where(gt1, first, jnp.where(gt2, x, second))
                first = jnp.where(gt1, x, first)
                lost = jnp.where(out > lost, out, lost)
            return first, second, third, lost

        lowest = jnp.full((sub, LANES), I16_MIN, jnp.int16)
        first, second, third, lost = lax.fori_loop(0, nch, body, (lowest,) * 4)
        top_scr[0, rr:rr + sub, :] = first
        top_scr[1, rr:rr + sub, :] = second
        top_scr[2, rr:rr + sub, :] = third
        return jnp.max(lost.astype(F32))

    lost_max = collect(0)
    for rr in range(sub, qblk, sub):
        lost_max = jnp.maximum(lost_max, collect(rr))
    lossless = lost_max == float(I16_MIN)

    def counts_top(cands):
        outs = []
        for r0, cand in zip(groups, cands):
            acc = jnp.zeros((rows, LANES), BF16)
            for t in range(3):
                acc = acc + jnp.where(top_scr[t, r0:r0 + rows, :] >= cand, one16, zero16)
            outs.append(jnp.dot(acc, lane_ones, preferred_element_type=F32))
        return outs

    def finish(thr_lo16, above_lo):
        for r0, lo16, quota, above in zip(groups, thr_lo16, quotas, above_lo):
            thr_scr[1, r0:r0 + rows, :] = lo16
            need_scr[r0:r0 + rows, :] = quota - above

    @pl.when(lossless)
    def _():
        finish(*bisect16(counts_top, quotas))

    @pl.when(jnp.logical_not(lossless))
    def _():
        def pin_body(j, carry):
            ks = pl.multiple_of(j * qblk, qblk)
            for r0, thr16 in zip(groups, thr_hi16):
                for c in range(nrep):
                    sl = (slice(r0, r0 + rows), pl.ds(ks + c * LANES, LANES))
                    lo_scr[sl] = jnp.where(hi_scr[sl] == thr16, lo_scr[sl], min16)
            return carry

        lax.fori_loop(0, nch, pin_body, 0)
        finish(*bisect16(lambda cands: counts16(lo_scr, cands), quotas))

    m_scr[...] = jnp.full(m_scr.shape, NEG_BIG, F32)
    acc_scr[...] = jnp.zeros(acc_scr.shape, F32)
    thr_hi_all = thr_scr[0]
    thr_lo_all = thr_scr[1]
    need_all = need_scr[...]
    nsub = MXU_DIM // LANES
    prefix_ones = jnp.where(lax.broadcasted_iota(jnp.int32, (MXU_DIM, MXU_DIM), 0)
                            <= lax.broadcasted_iota(jnp.int32, (MXU_DIM, MXU_DIM), 1), 1.0, 0.0).astype(BF16)
    ninf16 = jnp.full((), -jnp.inf, BF16)

    def attn_chunk(j, seen, diag):
        ks = pl.multiple_of(j * qblk, qblk)
        reach, equal = [], []
        for c in range(nrep):
            hi_c = hi_scr[:, pl.ds(ks + c * LANES, LANES)]
            lo_c = lo_scr[:, pl.ds(ks + c * LANES, LANES)]
            bucket = hi_c == thr_hi_all
            reach.append(jnp.where(hi_c > thr_hi_all, zero16,
                                   jnp.where(bucket, jnp.where(lo_c >= thr_lo_all, zero16, ninf16), ninf16)))
            equal.append(jnp.where(bucket, jnp.where(lo_c == thr_lo_all, one16, zero16), zero16))
        parts = []
        for u in range(qblk // MXU_DIM):
            eq_u = jnp.concatenate(equal[u * nsub:(u + 1) * nsub], axis=1)
            rank = jnp.dot(eq_u, prefix_ones, preferred_element_type=F32)
            room = jnp.concatenate([jnp.maximum(need_all - seen, 0.0)] * nsub, axis=1)
            over = jnp.where(rank * eq_u.astype(F32) > room, -jnp.inf, 0.0)
            parts.append(jnp.concatenate(reach[u * nsub:(u + 1) * nsub], axis=1).astype(F32) + over)
            seen = seen + jnp.max(rank, axis=-1, keepdims=True)
        bias = jnp.concatenate(parts, axis=1)
        if diag:
            bias = jnp.where(causal, bias, -jnp.inf)
        kc = k_ref[pl.ds(ks, qblk), :]
        vc = v_ref[pl.ds(ks, qblk), :]
        for h in range(ATT_HEADS):
            s = lax.dot_general(q_ref[h], kc, _NT, preferred_element_type=F32) + bias
            m_old = m_scr[h]
            m_new = jnp.maximum(m_old, jnp.max(s, axis=-1, keepdims=True))
            alpha = jnp.exp2(m_old - m_new)
            p = jnp.exp2(s - jnp.concatenate([m_new] * nrep, axis=1))
            acc_scr[h] = alpha * acc_scr[h] + jnp.dot(p.astype(BF16), vc, preferred_element_type=F32)
            m_scr[h] = m_new
        return seen

    seen = lax.fori_loop(0, qb, lambda j, seen: attn_chunk(j, seen, False), jnp.zeros((qblk, LANES), F32))
    attn_chunk(qb, seen, True)

    outs = []
    for h in range(ATT_HEADS):
        acc = acc_scr[h]
        outs.append(acc[:, :ATT_HEAD_DIM] / acc[:, ATT_HEAD_DIM:])
    o_ref[...] = jnp.concatenate(outs, axis=1).astype(o_ref.dtype)


def _dsa(q, qi, proj, ki, k, v, *, qblk=512, rows=256):
    b, _, s, _ = q.shape
    topk = min(TOPK_MAX, s // 4)
    assert qblk >= topk and s % qblk == 0 and qblk % rows == 0 and qblk % MXU_DIM == 0
    assert s // LANES <= 256
    kern = functools.partial(_dsa_kernel, qblk=qblk, topk=float(topk), rows=rows)
    head_tok = lambda bi, i: (bi, 0, i, 0)
    full = lambda bi, i: (bi, 0, 0)
    once = pl.Buffered(1)
    return pl.pallas_call(
        kern,
        grid=(b, s // qblk),
        in_specs=[
            pl.BlockSpec((None, ATT_HEADS, qblk, ATT_HEAD_DIM), head_tok),
            pl.BlockSpec((None, IDX_HEADS, qblk, IDX_DIM), head_tok),
            pl.BlockSpec((None, qblk, LANES), lambda bi, i: (bi, i, GATE_BLOCK)),
            pl.BlockSpec((None, s, IDX_DIM), full, pipeline_mode=once),
            pl.BlockSpec((None, s, ATT_HEAD_DIM), full, pipeline_mode=once),
            pl.BlockSpec((None, s, LANES), full, pipeline_mode=once),
        ],
        out_specs=pl.BlockSpec((None, qblk, ATT_WIDTH), lambda bi, i: (bi, i, 0)),
        out_shape=jax.ShapeDtypeStruct((b, s, ATT_WIDTH), BF16),
        scratch_shapes=[
            pltpu.VMEM((qblk, s + LANES), jnp.int16),
            pltpu.VMEM((qblk, s + LANES), jnp.int16),
            pltpu.VMEM((2, qblk, LANES), jnp.int16),
            pltpu.VMEM((3, qblk, LANES), jnp.int16),
            pltpu.VMEM((qblk, LANES), F32),
            pltpu.VMEM((ATT_HEADS, qblk, LANES), F32),
            pltpu.VMEM((ATT_HEADS, qblk, LANES), F32),
        ],
        compiler_params=_params("parallel", "arbitrary"),
        name="dsa_attention",
    )(q, qi, proj, ki, k, v)


def _log_sigmoid(x):
    return jnp.minimum(x, 0.0) - jnp.log(1.0 + jnp.exp(-jnp.abs(x)))


def _mlstm_kernel(qk_ref, v_ref, og_ref, g_ref, cw_ref, cb_ref, gb_ref, mn_ref, o_ref,
                  xe_scr, c_scr, n_scr, m_scr):
    L = CHUNK
    d = M_HEAD_DIM
    hi = lax.Precision.HIGHEST

    @pl.when(pl.program_id(1) == 0)
    def _():
        xe_scr[0:SUBLANES, :] = jnp.zeros((SUBLANES, 2 * M_WIDTH), F32)
        c_scr[...] = jnp.zeros(c_scr.shape, F32)
        n_scr[...] = jnp.zeros(n_scr.shape, F32)
        m_scr[...] = jnp.zeros(m_scr.shape, F32)

    xe_scr[SUBLANES:SUBLANES + L, :] = qk_ref[...]
    y = jnp.broadcast_to(cb_ref[...], (L, 2 * M_WIDTH))
    for j in range(CONV_K):
        off = SUBLANES - (CONV_K - 1) + j
        y = y + cw_ref[j:j + 1, :] * xe_scr[off:off + L, :]
    xe_scr[0:SUBLANES, :] = xe_scr[L:L + SUBLANES, :]
    qk = y * _sigmoid(y)

    g = g_ref[...] + gb_ref[...]
    ls = _log_sigmoid(g)
    row = lax.broadcasted_iota(jnp.int32, (L, L), 0)
    col = lax.broadcasted_iota(jnp.int32, (L, L), 1)
    lower = row >= col
    tril = jnp.where(lower, 1.0, 0.0)
    triu = jnp.where(row <= col, 1.0, 0.0)
    b_cols = jnp.dot(tril, ls, preferred_element_type=F32, precision=hi)
    g_t = g.T
    b_rows = jnp.dot(ls.T, triu, preferred_element_type=F32, precision=hi)

    for h in range(M_HEADS):
        sl = slice(h * d, (h + 1) * d)
        qh = qk[:, sl]
        kh = qk[:, M_WIDTH + h * d:M_WIDTH + (h + 1) * d] * (d ** -0.5)
        vh = v_ref[:, sl]
        qb16 = qh.astype(BF16)
        kb16 = kh.astype(BF16)
        vb16 = vh.astype(BF16)
        b_col = b_cols[:, F_LANE + h:F_LANE + h + 1]
        i_col = g[:, I_LANE + h:I_LANE + h + 1]
        b_row = b_rows[F_LANE + h:F_LANE + h + 1, :]
        i_row = g_t[I_LANE + h:I_LANE + h + 1, :]
        b_last = b_row[:, L - 1:L]

        c_prev = c_scr[h]
        n_prev = n_scr[h]
        m_prev = m_scr[h]

        a_row = b_last - b_row + i_row
        a_max = jnp.max(a_row, axis=-1, keepdims=True)
        wa_col = jnp.exp(b_last - b_col + i_col - a_max)
        kw = kh * wa_col
        c_chunk = lax.dot_general(kw.astype(BF16), vb16, _TN, preferred_element_type=F32)
        n_chunk = jnp.sum(kw, axis=0, keepdims=True)

        g_col = b_col + m_prev
        dmat = jnp.where(lower, b_col + (i_row - b_row), -jnp.inf)
        m_t = jnp.maximum(g_col, jnp.max(dmat, axis=-1, keepdims=True))
        w_intra = jnp.exp(dmat - m_t)
        w_inter = jnp.exp(g_col - m_t)
        s = lax.dot_general(qb16, kb16, _NT, preferred_element_type=F32) * w_intra
        num = w_inter * jnp.dot(qb16, c_prev.astype(BF16), preferred_element_type=F32) \
            + jnp.dot(s.astype(BF16), vb16, preferred_element_type=F32)
        den = w_inter * jnp.sum(qh * n_prev, axis=-1, keepdims=True) + jnp.sum(s, axis=-1, keepdims=True)
        hh = num / jnp.maximum(jnp.abs(den), jnp.exp(-m_t))
        hn = _rms_rows(hh, mn_ref[:, sl])
        o_ref[:, sl] = (hn * _sigmoid(og_ref[:, sl])).astype(o_ref.dtype)

        m_new = jnp.maximum(b_last + m_prev, a_max)
        s_old = jnp.exp(b_last + m_prev - m_new)
        s_new = jnp.exp(a_max - m_new)
        c_scr[h] = s_old * c_prev + s_new * c_chunk
        n_scr[h] = s_old * n_prev + s_new * n_chunk
        m_scr[h] = m_new


def _mlstm(proj, conv_w, conv_b, gate_b, m_norm):
    b, s, _ = proj.shape
    L = CHUNK
    cw = jnp.zeros((SUBLANES, 2 * M_WIDTH), F32).at[:CONV_K].set(conv_w.reshape(CONV_K, 2 * M_WIDTH))
    gb = jnp.zeros((1, LANES), F32).at[0, I_LANE:I_LANE + 2 * M_HEADS].set(gate_b)
    const = lambda bi, c: (0, 0)
    return pl.pallas_call(
        _mlstm_kernel,
        grid=(b, s // L),
        in_specs=[
            pl.BlockSpec((None, L, 2 * M_WIDTH), lambda bi, c: (bi, c, 1)),
            pl.BlockSpec((None, L, M_WIDTH), lambda bi, c: (bi, c, 4)),
            pl.BlockSpec((None, L, M_WIDTH), lambda bi, c: (bi, c, 5)),
            pl.BlockSpec((None, L, LANES), lambda bi, c: (bi, c, GATE_BLOCK)),
            pl.BlockSpec((SUBLANES, 2 * M_WIDTH), const),
            pl.BlockSpec((1, 2 * M_WIDTH), const),
            pl.BlockSpec((1, LANES), const),
            pl.BlockSpec((1, M_WIDTH), const),
        ],
        out_specs=pl.BlockSpec((None, L, M_WIDTH), lambda bi, c: (bi, c, 0)),
        out_shape=jax.ShapeDtypeStruct((b, s, M_WIDTH), BF16),
        scratch_shapes=[
            pltpu.VMEM((L + SUBLANES, 2 * M_WIDTH), F32),
            pltpu.VMEM((M_HEADS, M_HEAD_DIM, M_HEAD_DIM), F32),
            pltpu.VMEM((M_HEADS, 1, M_HEAD_DIM), F32),
            pltpu.VMEM((M_HEADS, 1, 1), F32),
        ],
        compiler_params=_params("parallel", "arbitrary"),
        name="mlstm",
    )(proj, proj, proj, proj, cw, conv_b.reshape(1, -1), gb, m_norm.reshape(1, M_WIDTH))


def _outproj_kernel(att_ref, hm_ref, x_ref, w_ref, o_ref):
    cat = jnp.concatenate([att_ref[...], hm_ref[...]], axis=1)
    o_ref[...] = x_ref[...] + jnp.dot(cat, w_ref[...], preferred_element_type=F32)


def _outproj(att2, hm2, x2, w_out, *, tm=512):
    t, d = x2.shape
    mix = att2.shape[1] + hm2.shape[1]
    return pl.pallas_call(
        _outproj_kernel,
        grid=(t // tm,),
        in_specs=[
            pl.BlockSpec((tm, att2.shape[1]), lambda i: (i, 0)),
            pl.BlockSpec((tm, hm2.shape[1]), lambda i: (i, 0)),
            pl.BlockSpec((tm, d), lambda i: (i, 0)),
            pl.BlockSpec((mix, d), lambda i: (0, 0), pipeline_mode=pl.Buffered(1)),
        ],
        out_specs=pl.BlockSpec((tm, d), lambda i: (i, 0)),
        out_shape=jax.ShapeDtypeStruct((t, d), F32),
        compiler_params=_params("parallel"),
        name="outproj",
    )(att2, hm2, x2, w_out.astype(BF16))


def kernel(x, ffn1_norm, ffn1_w_gate, ffn1_w_up, ffn1_w_down, mix_norm, w_in, conv_w, conv_b, gate_b,
           q_norm, k_norm, m_norm, w_out, ffn2_norm, ffn2_w_gate, ffn2_w_up, ffn2_w_down):
    b, s, d = x.shape
    depth = w_in.shape[0]
    x2 = x.reshape(b * s, d)
    for l in range(depth):
        x2 = _ffn(x2, ffn1_norm[l], ffn1_w_gate[l], ffn1_w_up[l], ffn1_w_down[l])
        proj = _inproj(x2, mix_norm[l], _arrange_w_in(w_in[l])).reshape(b, s, PROJ_COLS)
        q, qi, k, v, ki = _prep(proj, q_norm[l], k_norm[l])
        att = _dsa(q, qi, proj, ki, k, v)
        hm = _mlstm(proj, conv_w[l], conv_b[l], gate_b[l], m_norm[l])
        x2 = _outproj(att.reshape(b * s, ATT_WIDTH), hm.reshape(b * s, M_WIDTH), x2, w_out[l])
        x2 = _ffn(x2, ffn2_norm[l], ffn2_w_gate[l], ffn2_w_up[l], ffn2_w_down[l])
    return x2.reshape(b, s, d)
```

```python
import functools

import jax
import jax.numpy as jnp
from jax import lax
from jax.experimental import pallas as pl
from jax.experimental.pallas import tpu as pltpu

ATT_HEADS = 8
ATT_HEAD_DIM = 64
ATT_WIDTH = ATT_HEADS * ATT_HEAD_DIM
IDX_HEADS = 4
IDX_DIM = 64
TOPK_MAX = 256
M_HEADS = 4
M_HEAD_DIM = 128
M_WIDTH = M_HEADS * M_HEAD_DIM
CHUNK = 128
CONV_K = 4
ROPE_THETA = 10000.0
EPS = 1e-6

LANES = 128
SUBLANES = 8
MXU_DIM = 256
VMEM_LIMIT_BYTES = 56 * 1024 * 1024

PROJ_COLS = 3072
GATE_BLOCK = 7
W_LANE = 64
I_LANE = 68
F_LANE = 72

I16_MIN = -(2 ** 15)
NEG_BIG = -1e30
LOG2E = 1.4426950408889634

BF16 = jnp.bfloat16
F32 = jnp.float32

_NT = (((1,), (1,)), ((), ()))
_TN = (((0,), (0,)), ((), ()))


def _params(*sem):
    return pltpu.CompilerParams(dimension_semantics=sem, vmem_limit_bytes=VMEM_LIMIT_BYTES)


def _sigmoid(x):
    return 1.0 / (1.0 + jnp.exp(-x))


def _rms_rows(x, gain):
    return x * lax.rsqrt(jnp.mean(x * x, axis=-1, keepdims=True) + EPS) * gain


def _ffn_kernel(x_ref, g_ref, wg_ref, wu_ref, wd_ref, *rest, chunks):
    o_ref = rest[-1]
    x = x_ref[...]
    if len(rest) > 1:
        att_ref, hm_ref, wo_ref = rest[:-1]
        cat = jnp.concatenate([att_ref[...], hm_ref[...]], axis=1)
        x = x + jnp.dot(cat, wo_ref[...], preferred_element_type=F32)
    h = _rms_rows(x, g_ref[...]).astype(BF16)
    acc = jnp.zeros(x.shape, F32)
    f0 = 0
    for fc in chunks:
        g = jnp.dot(h, wg_ref[:, f0:f0 + fc], preferred_element_type=F32)
        u = jnp.dot(h, wu_ref[:, f0:f0 + fc], preferred_element_type=F32)
        a = (g * _sigmoid(g) * u).astype(BF16)
        acc = acc + jnp.dot(a, wd_ref[f0:f0 + fc, :], preferred_element_type=F32)
        f0 += fc
    o_ref[...] = x + 0.5 * acc


def _ffn_chunks(d_ff):
    step = 512
    chunks = [step] * (d_ff // step)
    if d_ff % step:
        chunks.append(d_ff % step)
    return tuple(chunks)


def _ffn(x2, gain, w_gate, w_up, w_down, mix=None, *, tm=512):
    t, d = x2.shape
    d_ff = w_gate.shape[1]
    const = lambda i: (0, 0)
    rows = lambda i: (i, 0)
    once = pl.Buffered(1)
    operands = [x2, gain.reshape(1, d), w_gate.astype(BF16), w_up.astype(BF16), w_down.astype(BF16)]
    in_specs = [
        pl.BlockSpec((tm, d), rows),
        pl.BlockSpec((1, d), const),
        pl.BlockSpec((d, d_ff), const, pipeline_mode=once),
        pl.BlockSpec((d, d_ff), const, pipeline_mode=once),
        pl.BlockSpec((d_ff, d), const, pipeline_mode=once),
    ]
    if mix is not None:
        att2, hm2, w_out = mix
        operands += [att2, hm2, w_out.astype(BF16)]
        in_specs += [
            pl.BlockSpec((tm, att2.shape[1]), rows),
            pl.BlockSpec((tm, hm2.shape[1]), rows),
            pl.BlockSpec((att2.shape[1] + hm2.shape[1], d), const, pipeline_mode=once),
        ]
    return pl.pallas_call(
        functools.partial(_ffn_kernel, chunks=_ffn_chunks(d_ff)),
        grid=(t // tm,),
        in_specs=in_specs,
        out_specs=pl.BlockSpec((tm, d), rows),
        out_shape=jax.ShapeDtypeStruct((t, d), F32),
        compiler_params=_params("parallel"),
        name="ffn" if mix is None else "mix_ffn",
    )(*operands)


def _inproj_kernel(x_ref, g_ref, w_ref, o_ref):
    h = _rms_rows(x_ref[...], g_ref[...]).astype(BF16)
    o_ref[...] = jnp.dot(h, w_ref[...], preferred_element_type=F32)


def _inproj(x2, gain, w_arranged, *, tm=512):
    t, d = x2.shape
    const = lambda i: (0, 0)
    return pl.pallas_call(
        _inproj_kernel,
        grid=(t // tm,),
        in_specs=[
            pl.BlockSpec((tm, d), lambda i: (i, 0)),
            pl.BlockSpec((1, d), const),
            pl.BlockSpec((d, PROJ_COLS), const, pipeline_mode=pl.Buffered(1)),
        ],
        out_specs=pl.BlockSpec((tm, PROJ_COLS), lambda i: (i, 0)),
        out_shape=jax.ShapeDtypeStruct((t, PROJ_COLS), F32),
        compiler_params=_params("parallel"),
        name="inproj",
    )(x2, gain.reshape(1, d), w_arranged)


def _arrange_w_in(w_in):
    d = w_in.shape[0]
    head = ATT_WIDTH + 2 * ATT_HEAD_DIM + IDX_HEADS * IDX_DIM + IDX_DIM + IDX_HEADS
    body = 4 * M_WIDTH
    gates = w_in[:, head + body:head + body + 2 * M_HEADS]
    pad = jnp.zeros((d, PROJ_COLS - head - body - 2 * M_HEADS), w_in.dtype)
    return jnp.concatenate([w_in[:, :head], gates, pad, w_in[:, head:head + body]], axis=1).astype(BF16)


def _rope(x, cosf, sinf):
    lane = lax.broadcasted_iota(jnp.int32, x.shape, 1)
    lower = (lane % ATT_HEAD_DIM) < (ATT_HEAD_DIM // 2)
    width = x.shape[1]
    partner = jnp.where(lower, pltpu.roll(x, width - ATT_HEAD_DIM // 2, 1), pltpu.roll(x, ATT_HEAD_DIM // 2, 1))
    return x * cosf + partner * sinf


def _group_mean_sq(x, gmat):
    sq = x * x
    hi = sq.astype(BF16)
    lo = (sq - hi.astype(F32)).astype(BF16)
    tot = jnp.dot(hi, gmat, preferred_element_type=F32) + jnp.dot(lo, gmat, preferred_element_type=F32)
    return tot * (1.0 / ATT_HEAD_DIM)


def _prep_kernel(p_ref, cos_ref, sin_ref, gmat_ref, qn_ref, kn_ref,
                 q_ref, qi_ref, k_ref, v_ref, ki_ref):
    cosf = cos_ref[...]
    sinf = sin_ref[...]
    gmat = gmat_ref[...]
    qk_scale = ATT_HEAD_DIM ** -0.5 * LOG2E
    idx_scale = IDX_DIM ** -0.5
    for s in range(ATT_WIDTH // LANES):
        x = p_ref[:, s * LANES:(s + 1) * LANES]
        xn = x * lax.rsqrt(_group_mean_sq(x, gmat) + EPS) * qn_ref[...]
        xr = (_rope(xn, cosf, sinf) * qk_scale).astype(BF16)
        q_ref[2 * s] = xr[:, :ATT_HEAD_DIM]
        q_ref[2 * s + 1] = xr[:, ATT_HEAD_DIM:]
    kv = p_ref[:, ATT_WIDTH:ATT_WIDTH + LANES]
    kn = kv * lax.rsqrt(_group_mean_sq(kv, gmat) + EPS) * kn_ref[...]
    kr = _rope(kn, cosf, sinf)
    k_ref[...] = kr[:, :ATT_HEAD_DIM].astype(BF16)
    lane = lax.broadcasted_iota(jnp.int32, kv.shape, 1)
    v_ref[...] = jnp.where(lane < ATT_HEAD_DIM, pltpu.roll(kv, ATT_HEAD_DIM, 1), 1.0).astype(BF16)
    base = ATT_WIDTH + LANES
    for s in range(IDX_HEADS * IDX_DIM // LANES):
        x = p_ref[:, base + s * LANES:base + (s + 1) * LANES]
        xr = (_rope(x, cosf, sinf) * idx_scale).astype(BF16)
        qi_ref[2 * s] = xr[:, :IDX_DIM]
        qi_ref[2 * s + 1] = xr[:, IDX_DIM:]
    gk = p_ref[:, GATE_BLOCK * LANES:(GATE_BLOCK + 1) * LANES]
    ki_ref[...] = _rope(gk, cosf, sinf)[:, :IDX_DIM].astype(BF16)


def _rope_tables(seq):
    half = ATT_HEAD_DIM // 2
    inv_freq = ROPE_THETA ** (-jnp.arange(half, dtype=F32) / half)
    ang = jnp.arange(seq, dtype=jnp.int32).astype(F32)[:, None] * inv_freq[None, :]
    cos = jnp.cos(ang)
    sin = jnp.sin(ang)
    cosf = jnp.concatenate([cos, cos, cos, cos], axis=1)
    sinf = jnp.concatenate([-sin, sin, -sin, sin], axis=1)
    return cosf, sinf


def _prep(proj, q_norm, k_norm, *, tm=512):
    b, s, _ = proj.shape
    cosf, sinf = _rope_tables(s)
    lane = jnp.arange(LANES)
    gmat = (lane[:, None] // ATT_HEAD_DIM == lane[None, :] // ATT_HEAD_DIM).astype(BF16)
    qn = jnp.concatenate([q_norm, q_norm]).reshape(1, LANES)
    kn = jnp.concatenate([k_norm, jnp.ones_like(k_norm)]).reshape(1, LANES)
    const = lambda bi, i: (0, 0)
    tok = lambda bi, i: (bi, i, 0)
    head_tok = lambda bi, i: (bi, 0, i, 0)
    return pl.pallas_call(
        _prep_kernel,
        grid=(b, s // tm),
        in_specs=[
            pl.BlockSpec((None, tm, 8 * LANES), tok),
            pl.BlockSpec((tm, LANES), lambda bi, i: (i, 0)),
            pl.BlockSpec((tm, LANES), lambda bi, i: (i, 0)),
            pl.BlockSpec((LANES, LANES), const),
            pl.BlockSpec((1, LANES), const),
            pl.BlockSpec((1, LANES), const),
        ],
        out_specs=[
            pl.BlockSpec((None, ATT_HEADS, tm, ATT_HEAD_DIM), head_tok),
            pl.BlockSpec((None, IDX_HEADS, tm, IDX_DIM), head_tok),
            pl.BlockSpec((None, tm, ATT_HEAD_DIM), tok),
            pl.BlockSpec((None, tm, LANES), tok),
            pl.BlockSpec((None, tm, IDX_DIM), tok),
        ],
        out_shape=[
            jax.ShapeDtypeStruct((b, ATT_HEADS, s, ATT_HEAD_DIM), BF16),
            jax.ShapeDtypeStruct((b, IDX_HEADS, s, IDX_DIM), BF16),
            jax.ShapeDtypeStruct((b, s, ATT_HEAD_DIM), BF16),
            jax.ShapeDtypeStruct((b, s, LANES), BF16),
            jax.ShapeDtypeStruct((b, s, IDX_DIM), BF16),
        ],
        compiler_params=_params("parallel", "parallel"),
        name="dsa_prep",
    )(proj, cosf, sinf, gmat, qn, kn)


def _sortable_key(score):
    bits = lax.bitcast_convert_type(score, jnp.int32)
    sign = bits >> 31
    return (bits ^ (sign & jnp.int32(0x7FFFFFFF))) - sign


def _dsa_kernel(q_ref, qi_ref, w_ref, ki_ref, k_ref, v_ref, o_ref,
                hi_scr, lo_scr, thr_scr, top_scr, need_scr, m_scr, acc_scr, *, qblk, topk, rows):
    qb = pl.program_id(1)
    nch = qb + 1
    nrep = qblk // LANES
    row = lax.broadcasted_iota(jnp.int32, (qblk, qblk), 0)
    col = lax.broadcasted_iota(jnp.int32, (qblk, qblk), 1)
    causal = col <= row

    wblk = w_ref[...]
    wcols = [jnp.broadcast_to(wblk[:, W_LANE + h:W_LANE + h + 1] * (IDX_HEADS ** -0.5), (qblk, qblk))
             for h in range(IDX_HEADS)]

    def score_chunk(j, diag):
        ks = pl.multiple_of(j * qblk, qblk)
        kc = ki_ref[pl.ds(ks, qblk), :]
        sc = jnp.zeros((qblk, qblk), F32)
        for h in range(IDX_HEADS):
            logit = lax.dot_general(qi_ref[h], kc, _NT, preferred_element_type=F32)
            sc = sc + jnp.maximum(logit, 0.0) * wcols[h]
        if diag:
            sc = jnp.where(causal, sc, -jnp.inf)
        key = _sortable_key(sc)
        hi_scr[:, pl.ds(ks, qblk)] = (key >> 16).astype(jnp.int16)
        lo_scr[:, pl.ds(ks, qblk)] = ((key & 0xFFFF) + I16_MIN).astype(jnp.int16)

    def score_body(j, carry):
        score_chunk(j, False)
        return carry

    lax.fori_loop(0, qb, score_body, 0)
    score_chunk(qb, True)

    one16 = jnp.ones((), BF16)
    zero16 = jnp.zeros((), BF16)
    lane_ones = jnp.ones((LANES, LANES), BF16)

    groups = list(range(0, qblk, rows))

    def hits16(src_scr, r0, cand):
        def slab(acc, ks, c):
            blk = src_scr[r0:r0 + rows, pl.ds(ks + c * LANES, LANES)]
            return acc + jnp.where(blk >= cand, one16, zero16)

        def pair_body(j, acc):
            ks = pl.multiple_of(j * (2 * qblk), 2 * qblk)
            for c in range(2 * nrep):
                acc = slab(acc, ks, c)
            return acc

        def single_body(j, acc):
            ks = pl.multiple_of(j * qblk, qblk)
            for c in range(nrep):
                acc = slab(acc, ks, c)
            return acc

        npair = nch // 2
        acc = lax.fori_loop(0, npair, pair_body, jnp.zeros((rows, LANES), BF16))
        return lax.fori_loop(2 * npair, nch, single_body, acc)

    def counts16(src_scr, cands):
        accs = [hits16(src_scr, r0, cand) for r0, cand in zip(groups, cands)]
        return [jnp.dot(acc, lane_ones, preferred_element_type=F32) for acc in accs]

    def bisect16(count_fn, targets):
        targets16 = [jnp.broadcast_to(jnp.asarray(target, F32), (rows, LANES)).astype(BF16) for target in targets]

        def body(bi, carry):
            thrs, aboves = carry
            bit = jnp.where(bi == 0, jnp.int32(I16_MIN), lax.shift_left(jnp.int32(1), 15 - bi))
            step = jnp.full((rows, LANES), bit, jnp.int32).astype(jnp.int16)
            cands = [thr ^ step for thr in thrs]
            cnts = [cnt.astype(BF16) for cnt in count_fn(cands)]
            takes = [cnt >= target for cnt, target in zip(cnts, targets16)]
            return (tuple(jnp.where(take, cand, thr) for take, cand, thr in zip(takes, cands, thrs)),
                    tuple(jnp.where(take, above, cnt) for take, above, cnt in zip(takes, aboves, cnts)))
        start = (tuple(jnp.full((rows, LANES), I16_MIN, jnp.int16) for _ in groups),
                 tuple(jnp.zeros((rows, LANES), BF16) for _ in groups))
        thrs, aboves = lax.fori_loop(0, 16, body, start)
        return list(thrs), [above.astype(F32) for above in aboves]

    min16 = jnp.full((), I16_MIN, jnp.int16)
    sub = 16 * SUBLANES
    thr_hi16, above_hi = bisect16(lambda cands: counts16(hi_scr, cands), [topk] * len(groups))
    quotas = [topk - cnt for cnt in above_hi]
    for r0, hi16 in zip(groups, thr_hi16):
        thr_scr[0, r0:r0 + rows, :] = hi16

    def collect(rr):
        thr_sub = thr_scr[0, rr:rr + sub, :]

        def body(j, carry):
            first, second, third, lost = carry
            ks = pl.multiple_of(j * qblk, qblk)
            for c in range(nrep):
                sl = (slice(rr, rr + sub), pl.ds(ks + c * LANES, LANES))
                x = jnp.where(hi_scr[sl] == thr_sub, lo_scr[sl], min16)
                gt1, gt2, gt3 = x > first, x > second, x > third
                out = jnp.where(gt3, third, x)
                third = jnp.where(gt2, second, jnp.where(gt3, x, third))
                second = jnp.where(gt1, first, jnp.where(gt2, x, second))
                first = jnp.where(gt1, x, first)
                lost = jnp.where(out > lost, out, lost)
            return first, second, third, lost

        lowest = jnp.full((sub, LANES), I16_MIN, jnp.int16)
        first, second, third, lost = lax.fori_loop(0, nch, body, (lowest,) * 4)
        top_scr[0, rr:rr + sub, :] = first
        top_scr[1, rr:rr + sub, :] = second
        top_scr[2, rr:rr + sub, :] = third
        return jnp.max(lost.astype(F32))

    lost_max = collect(0)
    for rr in range(sub, qblk, sub):
        lost_max = jnp.maximum(lost_max, collect(rr))
    lossless = lost_max == float(I16_MIN)

    def counts_top(cands):
        outs = []
        for r0, cand in zip(groups, cands):
            acc = jnp.zeros((rows, LANES), BF16)
            for t in range(3):
                acc = acc + jnp.where(top_scr[t, r0:r0 + rows, :] >= cand, one16, zero16)
            outs.append(jnp.dot(acc, lane_ones, preferred_element_type=F32))
        return outs

    def finish(thr_lo16, above_lo):
        for r0, lo16, quota, above in zip(groups, thr_lo16, quotas, above_lo):
            thr_scr[1, r0:r0 + rows, :] = lo16
            need_scr[r0:r0 + rows, :] = quota - above

    @pl.when(lossless)
    def _():
        finish(*bisect16(counts_top, quotas))

    @pl.when(jnp.logical_not(lossless))
    def _():
        def pin_body(j, carry):
            ks = pl.multiple_of(j * qblk, qblk)
            for r0, thr16 in zip(groups, thr_hi16):
                for c in range(nrep):
                    sl = (slice(r0, r0 + rows), pl.ds(ks + c * LANES, LANES))
                    lo_scr[sl] = jnp.where(hi_scr[sl] == thr16, lo_scr[sl], min16)
            return carry

        lax.fori_loop(0, nch, pin_body, 0)
        finish(*bisect16(lambda cands: counts16(lo_scr, cands), quotas))

    m_scr[...] = jnp.full(m_scr.shape, NEG_BIG, F32)
    acc_scr[...] = jnp.zeros(acc_scr.shape, F32)
    thr_hi_all = thr_scr[0]
    thr_lo_all = thr_scr[1]
    need_all = need_scr[...]
    nsub = MXU_DIM // LANES
    prefix_ones = jnp.where(lax.broadcasted_iota(jnp.int32, (MXU_DIM, MXU_DIM), 0)
                            <= lax.broadcasted_iota(jnp.int32, (MXU_DIM, MXU_DIM), 1), 1.0, 0.0).astype(BF16)
    ninf16 = jnp.full((), -jnp.inf, BF16)

    def attn_chunk(j, seen, diag):
        ks = pl.multiple_of(j * qblk, qblk)
        reach, equal = [], []
        for c in range(nrep):
            hi_c = hi_scr[:, pl.ds(ks + c * LANES, LANES)]
            lo_c = lo_scr[:, pl.ds(ks + c * LANES, LANES)]
            bucket = hi_c == thr_hi_all
            reach.append(jnp.where(hi_c > thr_hi_all, zero16,
                                   jnp.where(bucket, jnp.where(lo_c >= thr_lo_all, zero16, ninf16), ninf16)))
            equal.append(jnp.where(bucket, jnp.where(lo_c == thr_lo_all, one16, zero16), zero16))
        parts = []
        for u in range(qblk // MXU_DIM):
            eq_u = jnp.concatenate(equal[u * nsub:(u + 1) * nsub], axis=1)
            rank = jnp.dot(eq_u, prefix_ones, preferred_element_type=F32)
            room = jnp.concatenate([jnp.maximum(need_all - seen, 0.0)] * nsub, axis=1)
            over = jnp.where(rank * eq_u.astype(F32) > room, -jnp.inf, 0.0)
            parts.append(jnp.concatenate(reach[u * nsub:(u + 1) * nsub], axis=1).astype(F32) + over)
            seen = seen + jnp.max(rank, axis=-1, keepdims=True)
        bias = jnp.concatenate(parts, axis=1)
        if diag:
            bias = jnp.where(causal, bias, -jnp.inf)
        kc = k_ref[pl.ds(ks, qblk), :]
        vc = v_ref[pl.ds(ks, qblk), :]
        for h in range(ATT_HEADS):
            s = lax.dot_general(q_ref[h], kc, _NT, preferred_element_type=F32) + bias
            m_old = m_scr[h]
            m_new = jnp.maximum(m_old, jnp.max(s, axis=-1, keepdims=True))
            alpha = jnp.exp2(m_old - m_new)
            p = jnp.exp2(s - jnp.concatenate([m_new] * nrep, axis=1))
            acc_scr[h] = alpha * acc_scr[h] + jnp.dot(p.astype(BF16), vc, preferred_element_type=F32)
            m_scr[h] = m_new
        return seen

    seen = lax.fori_loop(0, qb, lambda j, seen: attn_chunk(j, seen, False), jnp.zeros((qblk, LANES), F32))
    attn_chunk(qb, seen, True)

    outs = []
    for h in range(ATT_HEADS):
        acc = acc_scr[h]
        outs.append(acc[:, :ATT_HEAD_DIM] / acc[:, ATT_HEAD_DIM:])
    o_ref[...] = jnp.concatenate(outs, axis=1).astype(o_ref.dtype)


def _dsa(q, qi, proj, ki, k, v, *, qblk=512, rows=256):
    b, _, s, _ = q.shape
    topk = min(TOPK_MAX, s // 4)
    assert qblk >= topk and s % qblk == 0 and qblk % rows == 0 and qblk % MXU_DIM == 0
    assert s // LANES <= 256
    kern = functools.partial(_dsa_kernel, qblk=qblk, topk=float(topk), rows=rows)
    head_tok = lambda bi, i: (bi, 0, i, 0)
    full = lambda bi, i: (bi, 0, 0)
    once = pl.Buffered(1)
    return pl.pallas_call(
        kern,
        grid=(b, s // qblk),
        in_specs=[
            pl.BlockSpec((None, ATT_HEADS, qblk, ATT_HEAD_DIM), head_tok),
            pl.BlockSpec((None, IDX_HEADS, qblk, IDX_DIM), head_tok),
            pl.BlockSpec((None, qblk, LANES), lambda bi, i: (bi, i, GATE_BLOCK)),
            pl.BlockSpec((None, s, IDX_DIM), full, pipeline_mode=once),
            pl.BlockSpec((None, s, ATT_HEAD_DIM), full, pipeline_mode=once),
            pl.BlockSpec((None, s, LANES), full, pipeline_mode=once),
        ],
        out_specs=pl.BlockSpec((None, qblk, ATT_WIDTH), lambda bi, i: (bi, i, 0)),
        out_shape=jax.ShapeDtypeStruct((b, s, ATT_WIDTH), BF16),
        scratch_shapes=[
            pltpu.VMEM((qblk, s + LANES), jnp.int16),
            pltpu.VMEM((qblk, s + LANES), jnp.int16),
            pltpu.VMEM((2, qblk, LANES), jnp.int16),
            pltpu.VMEM((3, qblk, LANES), jnp.int16),
            pltpu.VMEM((qblk, LANES), F32),
            pltpu.VMEM((ATT_HEADS, qblk, LANES), F32),
            pltpu.VMEM((ATT_HEADS, qblk, LANES), F32),
        ],
        compiler_params=_params("parallel", "arbitrary"),
        name="dsa_attention",
    )(q, qi, proj, ki, k, v)


def _log_sigmoid(x):
    return jnp.minimum(x, 0.0) - jnp.log(1.0 + jnp.exp(-jnp.abs(x)))


def _mlstm_kernel(qk_ref, v_ref, og_ref, g_ref, cw_ref, cb_ref, gb_ref, mn_ref, o_ref,
                  xe_scr, qk_scr, c_scr, n_scr, m_scr, *, nsub):
    L = CHUNK
    span = nsub * L
    hi = lax.Precision.HIGHEST

    @pl.when(pl.program_id(1) == 0)
    def _():
        xe_scr[0:SUBLANES, :] = jnp.zeros((SUBLANES, 2 * M_WIDTH), F32)
        c_scr[...] = jnp.zeros(c_scr.shape, F32)
        n_scr[...] = jnp.zeros(n_scr.shape, F32)
        m_scr[...] = jnp.zeros(m_scr.shape, F32)

    xe_scr[SUBLANES:SUBLANES + span, :] = qk_ref[...]
    y = jnp.broadcast_to(cb_ref[...], (span, 2 * M_WIDTH))
    for j in range(CONV_K):
        off = SUBLANES - (CONV_K - 1) + j
        y = y + cw_ref[j:j + 1, :] * xe_scr[off:off + span, :]
    xe_scr[0:SUBLANES, :] = xe_scr[span:span + SUBLANES, :]
    qk_scr[...] = y * _sigmoid(y)

    row = lax.broadcasted_iota(jnp.int32, (L, L), 0)
    col = lax.broadcasted_iota(jnp.int32, (L, L), 1)
    lower = row >= col
    tril = jnp.where(lower, 1.0, 0.0)
    triu = jnp.where(row <= col, 1.0, 0.0)
    for c in range(nsub):
        _mlstm_chunk(slice(c * L, (c + 1) * L), qk_scr, v_ref, og_ref, g_ref, gb_ref, mn_ref, o_ref,
                     c_scr, n_scr, m_scr, lower, tril, triu, hi)


def _mlstm_chunk(rs, qk_scr, v_ref, og_ref, g_ref, gb_ref, mn_ref, o_ref, c_scr, n_scr, m_scr,
                 lower, tril, triu, hi):
    L = CHUNK
    d = M_HEAD_DIM
    g = g_ref[rs, :] + gb_ref[...]
    ls = _log_sigmoid(g)
    b_cols = jnp.dot(tril, ls, preferred_element_type=F32, precision=hi)
    g_t = g.T
    b_rows = jnp.dot(ls.T, triu, preferred_element_type=F32, precision=hi)

    for h in range(M_HEADS):
        sl = slice(h * d, (h + 1) * d)
        qh = qk_scr[rs, sl]
        kh = qk_scr[rs, M_WIDTH + h * d:M_WIDTH + (h + 1) * d] * (d ** -0.5)
        vh = v_ref[rs, sl]
        qb16 = qh.astype(BF16)
        kb16 = kh.astype(BF16)
        vb16 = vh.astype(BF16)
        b_col = b_cols[:, F_LANE + h:F_LANE + h + 1]
        i_col = g[:, I_LANE + h:I_LANE + h + 1]
        b_row = b_rows[F_LANE + h:F_LANE + h + 1, :]
        i_row = g_t[I_LANE + h:I_LANE + h + 1, :]
        b_last = b_row[:, L - 1:L]

        c_prev = c_scr[h]
        n_prev = n_scr[h]
        m_prev = m_scr[h]

        a_row = b_last - b_row + i_row
        a_max = jnp.max(a_row, axis=-1, keepdims=True)
        wa_col = jnp.exp(b_last - b_col + i_col - a_max)
        kw = kh * wa_col
        c_chunk = lax.dot_general(kw.astype(BF16), vb16, _TN, preferred_element_type=F32)
        n_chunk = jnp.sum(kw, axis=0, keepdims=True)

        g_col = b_col + m_prev
        dmat = jnp.where(lower, b_col + (i_row - b_row), -jnp.inf)
        m_t = jnp.maximum(g_col, jnp.max(dmat, axis=-1, keepdims=True))
        w_intra = jnp.exp(dmat - m_t)
        w_inter = jnp.exp(g_col - m_t)
        s = lax.dot_general(qb16, kb16, _NT, preferred_element_type=F32) * w_intra
        num = w_inter * jnp.dot(qb16, c_prev.astype(BF16), preferred_element_type=F32) \
            + jnp.dot(s.astype(BF16), vb16, preferred_element_type=F32)
        den = w_inter * jnp.sum(qh * n_prev, axis=-1, keepdims=True) + jnp.sum(s, axis=-1, keepdims=True)
        hh = num / jnp.maximum(jnp.abs(den), jnp.exp(-m_t))
        hn = _rms_rows(hh, mn_ref[:, sl])
        o_ref[rs, sl] = (hn * _sigmoid(og_ref[rs, sl])).astype(o_ref.dtype)

        m_new = jnp.maximum(b_last + m_prev, a_max)
        s_old = jnp.exp(b_last + m_prev - m_new)
        s_new = jnp.exp(a_max - m_new)
        c_scr[h] = s_old * c_prev + s_new * c_chunk
        n_scr[h] = s_old * n_prev + s_new * n_chunk
        m_scr[h] = m_new


def _mlstm(proj, conv_w, conv_b, gate_b, m_norm, *, nsub=1):
    b, s, _ = proj.shape
    L = nsub * CHUNK
    assert s % L == 0
    cw = jnp.zeros((SUBLANES, 2 * M_WIDTH), F32).at[:CONV_K].set(conv_w.reshape(CONV_K, 2 * M_WIDTH))
    gb = jnp.zeros((1, LANES), F32).at[0, I_LANE:I_LANE + 2 * M_HEADS].set(gate_b)
    const = lambda bi, c: (0, 0)
    return pl.pallas_call(
        functools.partial(_mlstm_kernel, nsub=nsub),
        grid=(b, s // L),
        in_specs=[
            pl.BlockSpec((None, L, 2 * M_WIDTH), lambda bi, c: (bi, c, 1)),
            pl.BlockSpec((None, L, M_WIDTH), lambda bi, c: (bi, c, 4)),
            pl.BlockSpec((None, L, M_WIDTH), lambda bi, c: (bi, c, 5)),
            pl.BlockSpec((None, L, LANES), lambda bi, c: (bi, c, GATE_BLOCK)),
            pl.BlockSpec((SUBLANES, 2 * M_WIDTH), const),
            pl.BlockSpec((1, 2 * M_WIDTH), const),
            pl.BlockSpec((1, LANES), const),
            pl.BlockSpec((1, M_WIDTH), const),
        ],
        out_specs=pl.BlockSpec((None, L, M_WIDTH), lambda bi, c: (bi, c, 0)),
        out_shape=jax.ShapeDtypeStruct((b, s, M_WIDTH), BF16),
        scratch_shapes=[
            pltpu.VMEM((L + SUBLANES, 2 * M_WIDTH), F32),
            pltpu.VMEM((L, 2 * M_WIDTH), F32),
            pltpu.VMEM((M_HEADS, M_HEAD_DIM, M_HEAD_DIM), F32),
            pltpu.VMEM((M_HEADS, 1, M_HEAD_DIM), F32),
            pltpu.VMEM((M_HEADS, 1, 1), F32),
        ],
        compiler_params=_params("parallel", "arbitrary"),
        name="mlstm",
    )(proj, proj, proj, proj, cw, conv_b.reshape(1, -1), gb, m_norm.reshape(1, M_WIDTH))


def kernel(x, ffn1_norm, ffn1_w_gate, ffn1_w_up, ffn1_w_down, mix_norm, w_in, conv_w, conv_b, gate_b,
           q_norm, k_norm, m_norm, w_out, ffn2_norm, ffn2_w_gate, ffn2_w_up, ffn2_w_down):
    b, s, d = x.shape
    depth = w_in.shape[0]
    x2 = x.reshape(b * s, d)
    for l in range(depth):
        x2 = _ffn(x2, ffn1_norm[l], ffn1_w_gate[l], ffn1_w_up[l], ffn1_w_down[l])
        proj = _inproj(x2, mix_norm[l], _arrange_w_in(w_in[l])).reshape(b, s, PROJ_COLS)
        q, qi, k, v, ki = _prep(proj, q_norm[l], k_norm[l])
        att = _dsa(q, qi, proj, ki, k, v)
        hm = _mlstm(proj, conv_w[l], conv_b[l], gate_b[l], m_norm[l])
        mix = (att.reshape(b * s, ATT_WIDTH), hm.reshape(b * s, M_WIDTH), w_out[l])
        x2 = _ffn(x2, ffn2_norm[l], ffn2_w_gate[l], ffn2_w_up[l], ffn2_w_down[l], mix)
    return x2.reshape(b, s, d)
```

```python
import functools

import jax
import jax.numpy as jnp
from jax import lax
from jax.experimental import pallas as pl
from jax.experimental.pallas import tpu as pltpu

ATT_HEADS = 8
ATT_HEAD_DIM = 64
ATT_WIDTH = ATT_HEADS * ATT_HEAD_DIM
IDX_HEADS = 4
IDX_DIM = 64
TOPK_MAX = 256
M_HEADS = 4
M_HEAD_DIM = 128
M_WIDTH = M_HEADS * M_HEAD_DIM
CHUNK = 128
CONV_K = 4
ROPE_THETA = 10000.0
EPS = 1e-6

LANES = 128
SUBLANES = 8
MXU_DIM = 256
VMEM_LIMIT_BYTES = 56 * 1024 * 1024

PROJ_COLS = 3072
GATE_BLOCK = 7
W_LANE = 64
I_LANE = 68
F_LANE = 72

I16_MIN = -(2 ** 15)
NEG_BIG = -1e30
LOG2E = 1.4426950408889634

BF16 = jnp.bfloat16
F32 = jnp.float32

_NT = (((1,), (1,)), ((), ()))
_TN = (((0,), (0,)), ((), ()))


def _params(*sem):
    return pltpu.CompilerParams(dimension_semantics=sem, vmem_limit_bytes=VMEM_LIMIT_BYTES)


def _sigmoid(x):
    return 1.0 / (1.0 + jnp.exp(-x))


def _rms_rows(x, gain):
    return x * lax.rsqrt(jnp.mean(x * x, axis=-1, keepdims=True) + EPS) * gain


def _ffn_kernel(x_ref, g_ref, wg_ref, wu_ref, wd_ref, *rest, chunks):
    o_ref = rest[-1]
    x = x_ref[...]
    if len(rest) > 1:
        att_ref, hm_ref, wo_ref = rest[:-1]
        cat = jnp.concatenate([att_ref[...], hm_ref[...]], axis=1)
        x = x + jnp.dot(cat, wo_ref[...], preferred_element_type=F32)
    h = _rms_rows(x, g_ref[...]).astype(BF16)
    acc = jnp.zeros(x.shape, F32)
    f0 = 0
    for fc in chunks:
        g = jnp.dot(h, wg_ref[:, f0:f0 + fc], preferred_element_type=F32)
        u = jnp.dot(h, wu_ref[:, f0:f0 + fc], preferred_element_type=F32)
        a = (g * _sigmoid(g) * u).astype(BF16)
        acc = acc + jnp.dot(a, wd_ref[f0:f0 + fc, :], preferred_element_type=F32)
        f0 += fc
    o_ref[...] = x + 0.5 * acc


def _ffn_chunks(d_ff):
    step = 512
    chunks = [step] * (d_ff // step)
    if d_ff % step:
        chunks.append(d_ff % step)
    return tuple(chunks)


def _ffn(x2, gain, w_gate, w_up, w_down, mix=None, *, tm=512):
    t, d = x2.shape
    d_ff = w_gate.shape[1]
    const = lambda i: (0, 0)
    rows = lambda i: (i, 0)
    once = pl.Buffered(1)
    operands = [x2, gain.reshape(1, d), w_gate.astype(BF16), w_up.astype(BF16), w_down.astype(BF16)]
    in_specs = [
        pl.BlockSpec((tm, d), rows),
        pl.BlockSpec((1, d), const),
        pl.BlockSpec((d, d_ff), const, pipeline_mode=once),
        pl.BlockSpec((d, d_ff), const, pipeline_mode=once),
        pl.BlockSpec((d_ff, d), const, pipeline_mode=once),
    ]
    if mix is not None:
        att2, hm2, w_out = mix
        operands += [att2, hm2, w_out.astype(BF16)]
        in_specs += [
            pl.BlockSpec((tm, att2.shape[1]), rows),
            pl.BlockSpec((tm, hm2.shape[1]), rows),
            pl.BlockSpec((att2.shape[1] + hm2.shape[1], d), const, pipeline_mode=once),
        ]
    return pl.pallas_call(
        functools.partial(_ffn_kernel, chunks=_ffn_chunks(d_ff)),
        grid=(t // tm,),
        in_specs=in_specs,
        out_specs=pl.BlockSpec((tm, d), rows),
        out_shape=jax.ShapeDtypeStruct((t, d), F32),
        compiler_params=_params("parallel"),
        name="ffn" if mix is None else "mix_ffn",
    )(*operands)


def _inproj_kernel(x_ref, g_ref, w_ref, o_ref):
    h = _rms_rows(x_ref[...], g_ref[...]).astype(BF16)
    o_ref[...] = jnp.dot(h, w_ref[...], preferred_element_type=F32)


def _inproj(x2, gain, w_arranged, *, tm=512):
    t, d = x2.shape
    const = lambda i: (0, 0)
    return pl.pallas_call(
        _inproj_kernel,
        grid=(t // tm,),
        in_specs=[
            pl.BlockSpec((tm, d), lambda i: (i, 0)),
            pl.BlockSpec((1, d), const),
            pl.BlockSpec((d, PROJ_COLS), const, pipeline_mode=pl.Buffered(1)),
        ],
        out_specs=pl.BlockSpec((tm, PROJ_COLS), lambda i: (i, 0)),
        out_shape=jax.ShapeDtypeStruct((t, PROJ_COLS), F32),
        compiler_params=_params("parallel"),
        name="inproj",
    )(x2, gain.reshape(1, d), w_arranged)


def _arrange_w_in(w_in):
    d = w_in.shape[0]
    head = ATT_WIDTH + 2 * ATT_HEAD_DIM + IDX_HEADS * IDX_DIM + IDX_DIM + IDX_HEADS
    body = 4 * M_WIDTH
    gates = w_in[:, head + body:head + body + 2 * M_HEADS]
    pad = jnp.zeros((d, PROJ_COLS - head - body - 2 * M_HEADS), w_in.dtype)
    return jnp.concatenate([w_in[:, :head], gates, pad, w_in[:, head:head + body]], axis=1).astype(BF16)


def _rope(x, cosf, sinf):
    lane = lax.broadcasted_iota(jnp.int32, x.shape, 1)
    lower = (lane % ATT_HEAD_DIM) < (ATT_HEAD_DIM // 2)
    width = x.shape[1]
    partner = jnp.where(lower, pltpu.roll(x, width - ATT_HEAD_DIM // 2, 1), pltpu.roll(x, ATT_HEAD_DIM // 2, 1))
    return x * cosf + partner * sinf


def _group_mean_sq(x, gmat):
    sq = x * x
    hi = sq.astype(BF16)
    lo = (sq - hi.astype(F32)).astype(BF16)
    tot = jnp.dot(hi, gmat, preferred_element_type=F32) + jnp.dot(lo, gmat, preferred_element_type=F32)
    return tot * (1.0 / ATT_HEAD_DIM)


def _prep_kernel(p_ref, cos_ref, sin_ref, gmat_ref, qn_ref, kn_ref,
                 q_ref, qi_ref, k_ref, v_ref, ki_ref):
    cosf = cos_ref[...]
    sinf = sin_ref[...]
    gmat = gmat_ref[...]
    qk_scale = ATT_HEAD_DIM ** -0.5 * LOG2E
    idx_scale = IDX_DIM ** -0.5
    for s in range(ATT_WIDTH // LANES):
        x = p_ref[:, s * LANES:(s + 1) * LANES]
        xn = x * lax.rsqrt(_group_mean_sq(x, gmat) + EPS) * qn_ref[...]
        xr = (_rope(xn, cosf, sinf) * qk_scale).astype(BF16)
        q_ref[2 * s] = xr[:, :ATT_HEAD_DIM]
        q_ref[2 * s + 1] = xr[:, ATT_HEAD_DIM:]
    kv = p_ref[:, ATT_WIDTH:ATT_WIDTH + LANES]
    kn = kv * lax.rsqrt(_group_mean_sq(kv, gmat) + EPS) * kn_ref[...]
    kr = _rope(kn, cosf, sinf)
    k_ref[...] = kr[:, :ATT_HEAD_DIM].astype(BF16)
    lane = lax.broadcasted_iota(jnp.int32, kv.shape, 1)
    v_ref[...] = jnp.where(lane < ATT_HEAD_DIM, pltpu.roll(kv, ATT_HEAD_DIM, 1), 1.0).astype(BF16)
    base = ATT_WIDTH + LANES
    for s in range(IDX_HEADS * IDX_DIM // LANES):
        x = p_ref[:, base + s * LANES:base + (s + 1) * LANES]
        xr = (_rope(x, cosf, sinf) * idx_scale).astype(BF16)
        qi_ref[2 * s] = xr[:, :IDX_DIM]
        qi_ref[2 * s + 1] = xr[:, IDX_DIM:]
    gk = p_ref[:, GATE_BLOCK * LANES:(GATE_BLOCK + 1) * LANES]
    ki_ref[...] = _rope(gk, cosf, sinf)[:, :IDX_DIM].astype(BF16)


def _rope_tables(seq):
    half = ATT_HEAD_DIM // 2
    inv_freq = ROPE_THETA ** (-jnp.arange(half, dtype=F32) / half)
    ang = jnp.arange(seq, dtype=jnp.int32).astype(F32)[:, None] * inv_freq[None, :]
    cos = jnp.cos(ang)
    sin = jnp.sin(ang)
    cosf = jnp.concatenate([cos, cos, cos, cos], axis=1)
    sinf = jnp.concatenate([-sin, sin, -sin, sin], axis=1)
    return cosf, sinf


def _prep(proj, q_norm, k_norm, *, tm=512):
    b, s, _ = proj.shape
    cosf, sinf = _rope_tables(s)
    lane = jnp.arange(LANES)
    gmat = (lane[:, None] // ATT_HEAD_DIM == lane[None, :] // ATT_HEAD_DIM).astype(BF16)
    qn = jnp.concatenate([q_norm, q_norm]).reshape(1, LANES)
    kn = jnp.concatenate([k_norm, jnp.ones_like(k_norm)]).reshape(1, LANES)
    const = lambda bi, i: (0, 0)
    tok = lambda bi, i: (bi, i, 0)
    head_tok = lambda bi, i: (bi, 0, i, 0)
    return pl.pallas_call(
        _prep_kernel,
        grid=(b, s // tm),
        in_specs=[
            pl.BlockSpec((None, tm, 8 * LANES), tok),
            pl.BlockSpec((tm, LANES), lambda bi, i: (i, 0)),
            pl.BlockSpec((tm, LANES), lambda bi, i: (i, 0)),
            pl.BlockSpec((LANES, LANES), const),
            pl.BlockSpec((1, LANES), const),
            pl.BlockSpec((1, LANES), const),
        ],
        out_specs=[
            pl.BlockSpec((None, ATT_HEADS, tm, ATT_HEAD_DIM), head_tok),
            pl.BlockSpec((None, IDX_HEADS, tm, IDX_DIM), head_tok),
            pl.BlockSpec((None, tm, ATT_HEAD_DIM), tok),
            pl.BlockSpec((None, tm, LANES), tok),
            pl.BlockSpec((None, tm, IDX_DIM), tok),
        ],
        out_shape=[
            jax.ShapeDtypeStruct((b, ATT_HEADS, s, ATT_HEAD_DIM), BF16),
            jax.ShapeDtypeStruct((b, IDX_HEADS, s, IDX_DIM), BF16),
            jax.ShapeDtypeStruct((b, s, ATT_HEAD_DIM), BF16),
            jax.ShapeDtypeStruct((b, s, LANES), BF16),
            jax.ShapeDtypeStruct((b, s, IDX_DIM), BF16),
        ],
        compiler_params=_params("parallel", "parallel"),
        name="dsa_prep",
    )(proj, cosf, sinf, gmat, qn, kn)


def _sortable_key(score):
    bits = lax.bitcast_convert_type(score, jnp.int32)
    sign = bits >> 31
    return (bits ^ (sign & jnp.int32(0x7FFFFFFF))) - sign


def _dsa_kernel(q_ref, qi_ref, w_ref, ki_ref, k_ref, v_ref, o_ref,
                hi_scr, lo_scr, thr_scr, top_scr, need_scr, m_scr, acc_scr, *, qblk, topk, rows):
    qb = pl.program_id(1)
    nch = qb + 1
    nrep = qblk // LANES
    row = lax.broadcasted_iota(jnp.int32, (qblk, qblk), 0)
    col = lax.broadcasted_iota(jnp.int32, (qblk, qblk), 1)
    causal = col <= row

    wblk = w_ref[...]
    wcols = [jnp.broadcast_to(wblk[:, W_LANE + h:W_LANE + h + 1] * (IDX_HEADS ** -0.5), (qblk, qblk))
             for h in range(IDX_HEADS)]

    def score_chunk(j, diag):
        ks = pl.multiple_of(j * qblk, qblk)
        kc = ki_ref[pl.ds(ks, qblk), :]
        sc = jnp.zeros((qblk, qblk), F32)
        for h in range(IDX_HEADS):
            logit = lax.dot_general(qi_ref[h], kc, _NT, preferred_element_type=F32)
            sc = sc + jnp.maximum(logit, 0.0) * wcols[h]
        if diag:
            sc = jnp.where(causal, sc, -jnp.inf)
        key = _sortable_key(sc)
        hi_scr[:, pl.ds(ks, qblk)] = (key >> 16).astype(jnp.int16)
        lo_scr[:, pl.ds(ks, qblk)] = ((key & 0xFFFF) + I16_MIN).astype(jnp.int16)

    def score_body(j, carry):
        score_chunk(j, False)
        return carry

    lax.fori_loop(0, qb, score_body, 0)
    score_chunk(qb, True)

    one16 = jnp.ones((), BF16)
    zero16 = jnp.zeros((), BF16)
    lane_ones = jnp.ones((LANES, LANES), BF16)

    groups = list(range(0, qblk, rows))

    def hits16(src_scr, r0, cand):
        def slab(acc, ks, c):
            blk = src_scr[r0:r0 + rows, pl.ds(ks + c * LANES, LANES)]
            return acc + jnp.where(blk >= cand, one16, zero16)

        def pair_body(j, acc):
            ks = pl.multiple_of(j * (2 * qblk), 2 * qblk)
            for c in range(2 * nrep):
                acc = slab(acc, ks, c)
            return acc

        def single_body(j, acc):
            ks = pl.multiple_of(j * qblk, qblk)
            for c in range(nrep):
                acc = slab(acc, ks, c)
            return acc

        npair = nch // 2
        acc = lax.fori_loop(0, npair, pair_body, jnp.zeros((rows, LANES), BF16))
        return lax.fori_loop(2 * npair, nch, single_body, acc)

    def counts16(src_scr, cands):
        accs = [hits16(src_scr, r0, cand) for r0, cand in zip(groups, cands)]
        return [jnp.dot(acc, lane_ones, preferred_element_type=F32) for acc in accs]

    def bisect16(count_fn, targets):
        targets16 = [jnp.broadcast_to(jnp.asarray(target, F32), (rows, LANES)).astype(BF16) for target in targets]

        def body(bi, carry):
            thrs, aboves = carry
            bit = jnp.where(bi == 0, jnp.int32(I16_MIN), lax.shift_left(jnp.int32(1), 15 - bi))
            step = jnp.full((rows, LANES), bit, jnp.int32).astype(jnp.int16)
            cands = [thr ^ step for thr in thrs]
            cnts = [cnt.astype(BF16) for cnt in count_fn(cands)]
            takes = [cnt >= target for cnt, target in zip(cnts, targets16)]
            return (tuple(jnp.where(take, cand, thr) for take, cand, thr in zip(takes, cands, thrs)),
                    tuple(jnp.where(take, above, cnt) for take, above, cnt in zip(takes, aboves, cnts)))
        start = (tuple(jnp.full((rows, LANES), I16_MIN, jnp.int16) for _ in groups),
                 tuple(jnp.zeros((rows, LANES), BF16) for _ in groups))
        thrs, aboves = lax.fori_loop(0, 16, body, start)
        return list(thrs), [above.astype(F32) for above in aboves]

    min16 = jnp.full((), I16_MIN, jnp.int16)
    sub = 16 * SUBLANES
    thr_hi16, above_hi = bisect16(lambda cands: counts16(hi_scr, cands), [topk] * len(groups))
    quotas = [topk - cnt for cnt in above_hi]
    for r0, hi16 in zip(groups, thr_hi16):
        thr_scr[0, r0:r0 + rows, :] = hi16

    def collect(rr):
        thr_sub = thr_scr[0, rr:rr + sub, :]

        def body(j, carry):
            first, second, third, lost = carry
            ks = pl.multiple_of(j * qblk, qblk)
            for c in range(nrep):
                sl = (slice(rr, rr + sub), pl.ds(ks + c * LANES, LANES))
                x = jnp.where(hi_scr[sl] == thr_sub, lo_scr[sl], min16)
                gt1, gt2, gt3 = x > first, x > second, x > third
                out = jnp.where(gt3, third, x)
                third = jnp.where(gt2, second, jnp.where(gt3, x, third))
                second = jnp.where(gt1, first, jnp.where(gt2, x, second))
                first = jnp.where(gt1, x, first)
                lost = jnp.where(out > lost, out, lost)
            return first, second, third, lost

        lowest = jnp.full((sub, LANES), I16_MIN, jnp.int16)
        first, second, third, lost = lax.fori_loop(0, nch, body, (lowest,) * 4)
        top_scr[0, rr:rr + sub, :] = first
        top_scr[1, rr:rr + sub, :] = second
        top_scr[2, rr:rr + sub, :] = third
        return jnp.max(lost.astype(F32))

    lost_max = collect(0)
    for rr in range(sub, qblk, sub):
        lost_max = jnp.maximum(lost_max, collect(rr))
    lossless = lost_max == float(I16_MIN)

    def counts_top(cands):
        outs = []
        for r0, cand in zip(groups, cands):
            acc = jnp.zeros((rows, LANES), BF16)
            for t in range(3):
                acc = acc + jnp.where(top_scr[t, r0:r0 + rows, :] >= cand, one16, zero16)
            outs.append(jnp.dot(acc, lane_ones, preferred_element_type=F32))
        return outs

    def finish(thr_lo16, above_lo):
        for r0, lo16, quota, above in zip(groups, thr_lo16, quotas, above_lo):
            thr_scr[1, r0:r0 + rows, :] = lo16
            need_scr[r0:r0 + rows, :] = quota - above

    @pl.when(lossless)
    def _():
        finish(*bisect16(counts_top, quotas))

    @pl.when(jnp.logical_not(lossless))
    def _():
        def pin_body(j, carry):
            ks = pl.multiple_of(j * qblk, qblk)
            for r0, thr16 in zip(groups, thr_hi16):
                for c in range(nrep):
                    sl = (slice(r0, r0 + rows), pl.ds(ks + c * LANES, LANES))
                    lo_scr[sl] = jnp.where(hi_scr[sl] == thr16, lo_scr[sl], min16)
            return carry

        lax.fori_loop(0, nch, pin_body, 0)
        finish(*bisect16(lambda cands: counts16(lo_scr, cands), quotas))

    m_scr[...] = jnp.full(m_scr.shape, NEG_BIG, F32)
    acc_scr[...] = jnp.zeros(acc_scr.shape, F32)
    thr_hi_all = thr_scr[0]
    thr_lo_all = thr_scr[1]
    need_all = need_scr[...]
    nsub = MXU_DIM // LANES
    prefix_ones = jnp.where(lax.broadcasted_iota(jnp.int32, (MXU_DIM, MXU_DIM), 0)
                            <= lax.broadcasted_iota(jnp.int32, (MXU_DIM, MXU_DIM), 1), 1.0, 0.0).astype(BF16)
    ninf16 = jnp.full((), -jnp.inf, BF16)

    def attn_chunk(j, seen, diag):
        ks = pl.multiple_of(j * qblk, qblk)
        reach, equal = [], []
        for c in range(nrep):
            hi_c = hi_scr[:, pl.ds(ks + c * LANES, LANES)]
            lo_c = lo_scr[:, pl.ds(ks + c * LANES, LANES)]
            bucket = hi_c == thr_hi_all
            reach.append(jnp.where(hi_c > thr_hi_all, zero16,
                                   jnp.where(bucket, jnp.where(lo_c >= thr_lo_all, zero16, ninf16), ninf16)))
            equal.append(jnp.where(bucket, jnp.where(lo_c == thr_lo_all, one16, zero16), zero16))
        parts = []
        for u in range(qblk // MXU_DIM):
            eq_u = jnp.concatenate(equal[u * nsub:(u + 1) * nsub], axis=1)
            rank = jnp.dot(eq_u, prefix_ones, preferred_element_type=F32)
            room = jnp.concatenate([jnp.maximum(need_all - seen, 0.0)] * nsub, axis=1)
            over = jnp.where(rank * eq_u.astype(F32) > room, -jnp.inf, 0.0)
            parts.append(jnp.concatenate(reach[u * nsub:(u + 1) * nsub], axis=1).astype(F32) + over)
            seen = seen + jnp.max(rank, axis=-1, keepdims=True)
        bias = jnp.concatenate(parts, axis=1)
        if diag:
            bias = jnp.where(causal, bias, -jnp.inf)
        kc = k_ref[pl.ds(ks, qblk), :]
        vc = v_ref[pl.ds(ks, qblk), :]
        for h in range(ATT_HEADS):
            s = lax.dot_general(q_ref[h], kc, _NT, preferred_element_type=F32) + bias
            m_old = m_scr[h]
            m_new = jnp.maximum(m_old, jnp.max(s, axis=-1, keepdims=True))
            alpha = jnp.exp2(m_old - m_new)
            p = jnp.exp2(s - jnp.concatenate([m_new] * nrep, axis=1))
            acc_scr[h] = alpha * acc_scr[h] + jnp.dot(p.astype(BF16), vc, preferred_element_type=F32)
            m_scr[h] = m_new
        return seen

    seen = lax.fori_loop(0, qb, lambda j, seen: attn_chunk(j, seen, False), jnp.zeros((qblk, LANES), F32))
    attn_chunk(qb, seen, True)

    outs = []
    for h in range(ATT_HEADS):
        acc = acc_scr[h]
        outs.append(acc[:, :ATT_HEAD_DIM] / acc[:, ATT_HEAD_DIM:])
    o_ref[...] = jnp.concatenate(outs, axis=1).astype(o_ref.dtype)


def _dsa(q, qi, proj, ki, k, v, *, qblk=512, rows=256):
    b, _, s, _ = q.shape
    topk = min(TOPK_MAX, s // 4)
    assert qblk >= topk and s % qblk == 0 and qblk % rows == 0 and qblk % MXU_DIM == 0
    assert s // LANES <= 256
    kern = functools.partial(_dsa_kernel, qblk=qblk, topk=float(topk), rows=rows)
    head_tok = lambda bi, i: (bi, 0, i, 0)
    full = lambda bi, i: (bi, 0, 0)
    once = pl.Buffered(1)
    return pl.pallas_call(
        kern,
        grid=(b, s // qblk),
        in_specs=[
            pl.BlockSpec((None, ATT_HEADS, qblk, ATT_HEAD_DIM), head_tok),
            pl.BlockSpec((None, IDX_HEADS, qblk, IDX_DIM), head_tok),
            pl.BlockSpec((None, qblk, LANES), lambda bi, i: (bi, i, GATE_BLOCK)),
            pl.BlockSpec((None, s, IDX_DIM), full, pipeline_mode=once),
            pl.BlockSpec((None, s, ATT_HEAD_DIM), full, pipeline_mode=once),
            pl.BlockSpec((None, s, LANES), full, pipeline_mode=once),
        ],
        out_specs=pl.BlockSpec((None, qblk, ATT_WIDTH), lambda bi, i: (bi, i, 0)),
        out_shape=jax.ShapeDtypeStruct((b, s, ATT_WIDTH), BF16),
        scratch_shapes=[
            pltpu.VMEM((qblk, s + LANES), jnp.int16),
            pltpu.VMEM((qblk, s + LANES), jnp.int16),
            pltpu.VMEM((2, qblk, LANES), jnp.int16),
            pltpu.VMEM((3, qblk, LANES), jnp.int16),
            pltpu.VMEM((qblk, LANES), F32),
            pltpu.VMEM((ATT_HEADS, qblk, LANES), F32),
            pltpu.VMEM((ATT_HEADS, qblk, LANES), F32),
        ],
        compiler_params=_params("parallel", "arbitrary"),
        name="dsa_attention",
    )(q, qi, proj, ki, k, v)


def _log_sigmoid(x):
    return jnp.minimum(x, 0.0) - jnp.log(1.0 + jnp.exp(-jnp.abs(x)))


def _mlstm_kernel(qk_ref, v_ref, og_ref, g_ref, cw_ref, cb_ref, gb_ref, mn_ref, o_ref,
                  xe_scr, qk_scr, c_scr, n_scr, m_scr):
    L = CHUNK
    d = M_HEAD_DIM
    hi = lax.Precision.HIGHEST

    @pl.when(pl.program_id(1) == 0)
    def _():
        xe_scr[0:SUBLANES, :] = jnp.zeros((SUBLANES, 2 * M_WIDTH), F32)
        c_scr[...] = jnp.zeros(c_scr.shape, F32)
        n_scr[...] = jnp.zeros(n_scr.shape, F32)
        m_scr[...] = jnp.zeros(m_scr.shape, F32)

    xe_scr[SUBLANES:SUBLANES + L, :] = qk_ref[...]
    y = jnp.broadcast_to(cb_ref[...], (L, 2 * M_WIDTH))
    for j in range(CONV_K):
        off = SUBLANES - (CONV_K - 1) + j
        y = y + cw_ref[j:j + 1, :] * xe_scr[off:off + L, :]
    xe_scr[0:SUBLANES, :] = xe_scr[L:L + SUBLANES, :]
    qk_scr[...] = y * _sigmoid(y)

    g = g_ref[...] + gb_ref[...]
    ls = _log_sigmoid(g)
    row = lax.broadcasted_iota(jnp.int32, (L, L), 0)
    col = lax.broadcasted_iota(jnp.int32, (L, L), 1)
    lower = row >= col
    tril = jnp.where(lower, 1.0, 0.0)
    triu = jnp.where(row <= col, 1.0, 0.0)
    b_cols = jnp.dot(tril, ls, preferred_element_type=F32, precision=hi)
    g_t = g.T
    b_rows = jnp.dot(ls.T, triu, preferred_element_type=F32, precision=hi)

    for h in range(M_HEADS):
        sl = slice(h * d, (h + 1) * d)
        qh = qk_scr[:, sl]
        kh = qk_scr[:, M_WIDTH + h * d:M_WIDTH + (h + 1) * d] * (d ** -0.5)
        vh = v_ref[:, sl]
        qb16 = qh.astype(BF16)
        kb16 = kh.astype(BF16)
        vb16 = vh.astype(BF16)
        b_col = b_cols[:, F_LANE + h:F_LANE + h + 1]
        i_col = g[:, I_LANE + h:I_LANE + h + 1]
        b_row = b_rows[F_LANE + h:F_LANE + h + 1, :]
        i_row = g_t[I_LANE + h:I_LANE + h + 1, :]
        b_last = b_row[:, L - 1:L]

        c_prev = c_scr[h]
        n_prev = n_scr[h]
        m_prev = m_scr[h]

        a_row = b_last - b_row + i_row
        a_max = jnp.max(a_row, axis=-1, keepdims=True)
        wa_col = jnp.exp(b_last - b_col + i_col - a_max)
        kw = kh * wa_col
        c_chunk = lax.dot_general(kw.astype(BF16), vb16, _TN, preferred_element_type=F32)
        n_chunk = jnp.sum(kw, axis=0, keepdims=True)

        g_col = b_col + m_prev
        dmat = jnp.where(lower, b_col + (i_row - b_row), -jnp.inf)
        m_t = jnp.maximum(g_col, jnp.max(dmat, axis=-1, keepdims=True))
        w_intra = jnp.exp(dmat - m_t)
        w_inter = jnp.exp(g_col - m_t)
        s = lax.dot_general(qb16, kb16, _NT, preferred_element_type=F32) * w_intra
        num = w_inter * jnp.dot(qb16, c_prev.astype(BF16), preferred_element_type=F32) \
            + jnp.dot(s.astype(BF16), vb16, preferred_element_type=F32)
        den = w_inter * jnp.sum(qh * n_prev, axis=-1, keepdims=True) + jnp.sum(s, axis=-1, keepdims=True)
        hh = num / jnp.maximum(jnp.abs(den), jnp.exp(-m_t))
        hn = _rms_rows(hh, mn_ref[:, sl])
        o_ref[:, sl] = (hn * _sigmoid(og_ref[:, sl])).astype(o_ref.dtype)

        m_new = jnp.maximum(b_last + m_prev, a_max)
        s_old = jnp.exp(b_last + m_prev - m_new)
        s_new = jnp.exp(a_max - m_new)
        c_scr[h] = s_old * c_prev + s_new * c_chunk
        n_scr[h] = s_old * n_prev + s_new * n_chunk
        m_scr[h] = m_new


def _mlstm(proj, conv_w, conv_b, gate_b, m_norm):
    b, s, _ = proj.shape
    L = CHUNK
    assert s % L == 0
    cw = jnp.zeros((SUBLANES, 2 * M_WIDTH), F32).at[:CONV_K].set(conv_w.reshape(CONV_K, 2 * M_WIDTH))
    gb = jnp.zeros((1, LANES), F32).at[0, I_LANE:I_LANE + 2 * M_HEADS].set(gate_b)
    const = lambda bi, c: (0, 0)
    return pl.pallas_call(
        _mlstm_kernel,
        grid=(b, s // L),
        in_specs=[
            pl.BlockSpec((None, L, 2 * M_WIDTH), lambda bi, c: (bi, c, 1)),
            pl.BlockSpec((None, L, M_WIDTH), lambda bi, c: (bi, c, 4)),
            pl.BlockSpec((None, L, M_WIDTH), lambda bi, c: (bi, c, 5)),
            pl.BlockSpec((None, L, LANES), lambda bi, c: (bi, c, GATE_BLOCK)),
            pl.BlockSpec((SUBLANES, 2 * M_WIDTH), const),
            pl.BlockSpec((1, 2 * M_WIDTH), const),
            pl.BlockSpec((1, LANES), const),
            pl.BlockSpec((1, M_WIDTH), const),
        ],
        out_specs=pl.BlockSpec((None, L, M_WIDTH), lambda bi, c: (bi, c, 0)),
        out_shape=jax.ShapeDtypeStruct((b, s, M_WIDTH), BF16),
        scratch_shapes=[
            pltpu.VMEM((L + SUBLANES, 2 * M_WIDTH), F32),
            pltpu.VMEM((L, 2 * M_WIDTH), F32),
            pltpu.VMEM((M_HEADS, M_HEAD_DIM, M_HEAD_DIM), F32),
            pltpu.VMEM((M_HEADS, 1, M_HEAD_DIM), F32),
            pltpu.VMEM((M_HEADS, 1, 1), F32),
        ],
        compiler_params=_params("parallel", "arbitrary"),
        name="mlstm",
    )(proj, proj, proj, proj, cw, conv_b.reshape(1, -1), gb, m_norm.reshape(1, M_WIDTH))


def kernel(x, ffn1_norm, ffn1_w_gate, ffn1_w_up, ffn1_w_down, mix_norm, w_in, conv_w, conv_b, gate_b,
           q_norm, k_norm, m_norm, w_out, ffn2_norm, ffn2_w_gate, ffn2_w_up, ffn2_w_down):
    b, s, d = x.shape
    depth = w_in.shape[0]
    x2 = x.reshape(b * s, d)
    for l in range(depth):
        x2 = _ffn(x2, ffn1_norm[l], ffn1_w_gate[l], ffn1_w_up[l], ffn1_w_down[l])
        proj = _inproj(x2, mix_norm[l], _arrange_w_in(w_in[l])).reshape(b, s, PROJ_COLS)
        q, qi, k, v, ki = _prep(proj, q_norm[l], k_norm[l])
        att = _dsa(q, qi, proj, ki, k, v)
        hm = _mlstm(proj, conv_w[l], conv_b[l], gate_b[l], m_norm[l])
        mix = (att.reshape(b * s, ATT_WIDTH), hm.reshape(b * s, M_WIDTH), w_out[l])
        x2 = _ffn(x2, ffn2_norm[l], ffn2_w_gate[l], ffn2_w_up[l], ffn2_w_down[l], mix)
    return x2.reshape(b, s, d)
```

```python
import functools

import jax
import jax.numpy as jnp
from jax import lax
from jax.experimental import pallas as pl
from jax.experimental.pallas import tpu as pltpu

ATT_HEADS = 8
ATT_HEAD_DIM = 64
ATT_WIDTH = ATT_HEADS * ATT_HEAD_DIM
IDX_HEADS = 4
IDX_DIM = 64
TOPK_MAX = 256
M_HEADS = 4
M_HEAD_DIM = 128
M_WIDTH = M_HEADS * M_HEAD_DIM
CHUNK = 128
CONV_K = 4
ROPE_THETA = 10000.0
EPS = 1e-6

LANES = 128
SUBLANES = 8
MXU_DIM = 256
VMEM_LIMIT_BYTES = 56 * 1024 * 1024

PROJ_COLS = 3072
DSA_COLS = 1024
GATE_BLOCK = 7
W_LANE = 64
I_LANE = 68
F_LANE = 72

I16_MIN = -(2 ** 15)
NEG_BIG = -1e30
LOG2E = 1.4426950408889634

BF16 = jnp.bfloat16
F32 = jnp.float32

_NT = (((1,), (1,)), ((), ()))
_TN = (((0,), (0,)), ((), ()))


def _params(*sem):
    return pltpu.CompilerParams(dimension_semantics=sem, vmem_limit_bytes=VMEM_LIMIT_BYTES)


def _sigmoid(x):
    return 1.0 / (1.0 + jnp.exp(-x))


def _rms_rows(x, gain):
    return x * lax.rsqrt(jnp.mean(x * x, axis=-1, keepdims=True) + EPS) * gain


def _ffn_kernel(x_ref, g_ref, wg_ref, wu_ref, wd_ref, *rest, chunks):
    o_ref = rest[-1]
    x = x_ref[...]
    if len(rest) > 1:
        att_ref, hm_ref, wo_ref = rest[:-1]
        cat = jnp.concatenate([att_ref[...], hm_ref[...]], axis=1)
        x = x + jnp.dot(cat, wo_ref[...], preferred_element_type=F32)
    h = _rms_rows(x, g_ref[...]).astype(BF16)
    acc = jnp.zeros(x.shape, F32)
    f0 = 0
    for fc in chunks:
        g = jnp.dot(h, wg_ref[:, f0:f0 + fc], preferred_element_type=F32)
        u = jnp.dot(h, wu_ref[:, f0:f0 + fc], preferred_element_type=F32)
        a = (g * _sigmoid(g) * u).astype(BF16)
        acc = acc + jnp.dot(a, wd_ref[f0:f0 + fc, :], preferred_element_type=F32)
        f0 += fc
    o_ref[...] = x + 0.5 * acc


def _ffn_chunks(d_ff):
    step = 512
    chunks = [step] * (d_ff // step)
    if d_ff % step:
        chunks.append(d_ff % step)
    return tuple(chunks)


def _ffn(x2, gain, w_gate, w_up, w_down, mix=None, *, tm=512):
    t, d = x2.shape
    d_ff = w_gate.shape[1]
    const = lambda i: (0, 0)
    rows = lambda i: (i, 0)
    once = pl.Buffered(1)
    operands = [x2, gain.reshape(1, d), w_gate.astype(BF16), w_up.astype(BF16), w_down.astype(BF16)]
    in_specs = [
        pl.BlockSpec((tm, d), rows),
        pl.BlockSpec((1, d), const),
        pl.BlockSpec((d, d_ff), const, pipeline_mode=once),
        pl.BlockSpec((d, d_ff), const, pipeline_mode=once),
        pl.BlockSpec((d_ff, d), const, pipeline_mode=once),
    ]
    if mix is not None:
        att2, hm2, w_out = mix
        operands += [att2, hm2, w_out.astype(BF16)]
        in_specs += [
            pl.BlockSpec((tm, att2.shape[1]), rows),
            pl.BlockSpec((tm, hm2.shape[1]), rows),
            pl.BlockSpec((att2.shape[1] + hm2.shape[1], d), const, pipeline_mode=once),
        ]
    return pl.pallas_call(
        functools.partial(_ffn_kernel, chunks=_ffn_chunks(d_ff)),
        grid=(t // tm,),
        in_specs=in_specs,
        out_specs=pl.BlockSpec((tm, d), rows),
        out_shape=jax.ShapeDtypeStruct((t, d), F32),
        compiler_params=_params("parallel"),
        name="ffn" if mix is None else "mix_ffn",
    )(*operands)


def _inproj_kernel(x_ref, g_ref, w_ref, cos_ref, sin_ref, gmat_ref, qn_ref, kn_ref,
                   o_ref, q_ref, qi_ref, k_ref, v_ref, ki_ref):
    h = _rms_rows(x_ref[...], g_ref[...]).astype(BF16)
    head = DSA_COLS
    p = jnp.dot(h, w_ref[:, :head], preferred_element_type=F32)
    o_ref[:, :head] = p
    o_ref[:, head:] = jnp.dot(h, w_ref[:, head:], preferred_element_type=F32)
    _dsa_operands(p, cos_ref[...], sin_ref[...], gmat_ref[...], qn_ref[...], kn_ref[...],
                  q_ref, qi_ref, k_ref, v_ref, ki_ref)


def _inproj(x2, gain, w_arranged, q_norm, k_norm, b, s, *, tm=512):
    t, d = x2.shape
    assert s % tm == 0
    nblk = s // tm
    cosf, sinf = _rope_tables(s)
    lane = jnp.arange(LANES)
    gmat = (lane[:, None] // ATT_HEAD_DIM == lane[None, :] // ATT_HEAD_DIM).astype(BF16)
    qn = jnp.concatenate([q_norm, q_norm]).reshape(1, LANES)
    kn = jnp.concatenate([k_norm, jnp.ones_like(k_norm)]).reshape(1, LANES)
    const = lambda i: (0, 0)
    pos = lambda i: (i % nblk, 0)
    tok = lambda i: (i // nblk, i % nblk, 0)
    head_tok = lambda i: (i // nblk, 0, i % nblk, 0)
    return pl.pallas_call(
        _inproj_kernel,
        grid=(t // tm,),
        in_specs=[
            pl.BlockSpec((tm, d), lambda i: (i, 0)),
            pl.BlockSpec((1, d), const),
            pl.BlockSpec((d, PROJ_COLS), const, pipeline_mode=pl.Buffered(1)),
            pl.BlockSpec((tm, LANES), pos),
            pl.BlockSpec((tm, LANES), pos),
            pl.BlockSpec((LANES, LANES), const),
            pl.BlockSpec((1, LANES), const),
            pl.BlockSpec((1, LANES), const),
        ],
        out_specs=[
            pl.BlockSpec((tm, PROJ_COLS), lambda i: (i, 0)),
            pl.BlockSpec((None, ATT_HEADS, tm, ATT_HEAD_DIM), head_tok),
            pl.BlockSpec((None, IDX_HEADS, tm, IDX_DIM), head_tok),
            pl.BlockSpec((None, tm, ATT_HEAD_DIM), tok),
            pl.BlockSpec((None, tm, LANES), tok),
            pl.BlockSpec((None, tm, IDX_DIM), tok),
        ],
        out_shape=[
            jax.ShapeDtypeStruct((t, PROJ_COLS), F32),
            jax.ShapeDtypeStruct((b, ATT_HEADS, s, ATT_HEAD_DIM), BF16),
            jax.ShapeDtypeStruct((b, IDX_HEADS, s, IDX_DIM), BF16),
            jax.ShapeDtypeStruct((b, s, ATT_HEAD_DIM), BF16),
            jax.ShapeDtypeStruct((b, s, LANES), BF16),
            jax.ShapeDtypeStruct((b, s, IDX_DIM), BF16),
        ],
        compiler_params=_params("parallel"),
        name="inproj",
    )(x2, gain.reshape(1, d), w_arranged, cosf, sinf, gmat, qn, kn)


def _arrange_w_in(w_in):
    d = w_in.shape[0]
    head = ATT_WIDTH + 2 * ATT_HEAD_DIM + IDX_HEADS * IDX_DIM + IDX_DIM + IDX_HEADS
    body = 4 * M_WIDTH
    gates = w_in[:, head + body:head + body + 2 * M_HEADS]
    pad = jnp.zeros((d, PROJ_COLS - head - body - 2 * M_HEADS), w_in.dtype)
    return jnp.concatenate([w_in[:, :head], gates, pad, w_in[:, head:head + body]], axis=1).astype(BF16)


def _rope(x, cosf, sinf):
    lane = lax.broadcasted_iota(jnp.int32, x.shape, 1)
    lower = (lane % ATT_HEAD_DIM) < (ATT_HEAD_DIM // 2)
    width = x.shape[1]
    partner = jnp.where(lower, pltpu.roll(x, width - ATT_HEAD_DIM // 2, 1), pltpu.roll(x, ATT_HEAD_DIM // 2, 1))
    return x * cosf + partner * sinf


def _group_mean_sq(x, gmat):
    sq = x * x
    hi = sq.astype(BF16)
    lo = (sq - hi.astype(F32)).astype(BF16)
    tot = jnp.dot(hi, gmat, preferred_element_type=F32) + jnp.dot(lo, gmat, preferred_element_type=F32)
    return tot * (1.0 / ATT_HEAD_DIM)


def _dsa_operands(p, cosf, sinf, gmat, qn, kn, q_ref, qi_ref, k_ref, v_ref, ki_ref):
    qk_scale = ATT_HEAD_DIM ** -0.5 * LOG2E
    idx_scale = IDX_DIM ** -0.5
    for s in range(ATT_WIDTH // LANES):
        x = p[:, s * LANES:(s + 1) * LANES]
        xn = x * lax.rsqrt(_group_mean_sq(x, gmat) + EPS) * qn
        xr = (_rope(xn, cosf, sinf) * qk_scale).astype(BF16)
        q_ref[2 * s] = xr[:, :ATT_HEAD_DIM]
        q_ref[2 * s + 1] = xr[:, ATT_HEAD_DIM:]
    kv = p[:, ATT_WIDTH:ATT_WIDTH + LANES]
    kn = kv * lax.rsqrt(_group_mean_sq(kv, gmat) + EPS) * kn
    kr = _rope(kn, cosf, sinf)
    k_ref[...] = kr[:, :ATT_HEAD_DIM].astype(BF16)
    lane = lax.broadcasted_iota(jnp.int32, kv.shape, 1)
    v_ref[...] = jnp.where(lane < ATT_HEAD_DIM, pltpu.roll(kv, ATT_HEAD_DIM, 1), 1.0).astype(BF16)
    base = ATT_WIDTH + LANES
    for s in range(IDX_HEADS * IDX_DIM // LANES):
        x = p[:, base + s * LANES:base + (s + 1) * LANES]
        xr = (_rope(x, cosf, sinf) * idx_scale).astype(BF16)
        qi_ref[2 * s] = xr[:, :IDX_DIM]
        qi_ref[2 * s + 1] = xr[:, IDX_DIM:]
    gk = p[:, GATE_BLOCK * LANES:(GATE_BLOCK + 1) * LANES]
    ki_ref[...] = _rope(gk, cosf, sinf)[:, :IDX_DIM].astype(BF16)


def _rope_tables(seq):
    half = ATT_HEAD_DIM // 2
    inv_freq = ROPE_THETA ** (-jnp.arange(half, dtype=F32) / half)
    ang = jnp.arange(seq, dtype=jnp.int32).astype(F32)[:, None] * inv_freq[None, :]
    cos = jnp.cos(ang)
    sin = jnp.sin(ang)
    cosf = jnp.concatenate([cos, cos, cos, cos], axis=1)
    sinf = jnp.concatenate([-sin, sin, -sin, sin], axis=1)
    return cosf, sinf


def _sortable_key(score):
    bits = lax.bitcast_convert_type(score, jnp.int32)
    sign = bits >> 31
    return (bits ^ (sign & jnp.int32(0x7FFFFFFF))) - sign


def _dsa_kernel(q_ref, qi_ref, w_ref, ki_ref, k_ref, v_ref, o_ref,
                hi_scr, lo_scr, thr_scr, top_scr, need_scr, m_scr, acc_scr, *, qblk, topk, rows):
    qb = pl.program_id(1)
    nch = qb + 1
    nrep = qblk // LANES
    row = lax.broadcasted_iota(jnp.int32, (qblk, qblk), 0)
    col = lax.broadcasted_iota(jnp.int32, (qblk, qblk), 1)
    causal = col <= row

    wblk = w_ref[...]
    wcols = [jnp.broadcast_to(wblk[:, W_LANE + h:W_LANE + h + 1] * (IDX_HEADS ** -0.5), (qblk, qblk))
             for h in range(IDX_HEADS)]

    def score_chunk(j, diag):
        ks = pl.multiple_of(j * qblk, qblk)
        kc = ki_ref[pl.ds(ks, qblk), :]
        sc = jnp.zeros((qblk, qblk), F32)
        for h in range(IDX_HEADS):
            logit = lax.dot_general(qi_ref[h], kc, _NT, preferred_element_type=F32)
            sc = sc + jnp.maximum(logit, 0.0) * wcols[h]
        if diag:
            sc = jnp.where(causal, sc, -jnp.inf)
        key = _sortable_key(sc)
        hi_scr[:, pl.ds(ks, qblk)] = (key >> 16).astype(jnp.int16)
        lo_scr[:, pl.ds(ks, qblk)] = ((key & 0xFFFF) + I16_MIN).astype(jnp.int16)

    def score_body(j, carry):
        score_chunk(j, False)
        return carry

    lax.fori_loop(0, qb, score_body, 0)
    score_chunk(qb, True)

    one16 = jnp.ones((), BF16)
    zero16 = jnp.zeros((), BF16)
    lane_ones = jnp.ones((LANES, LANES), BF16)

    groups = list(range(0, qblk, rows))

    def hits16(src_scr, r0, cand):
        def slab(acc, ks, c):
            blk = src_scr[r0:r0 + rows, pl.ds(ks + c * LANES, LANES)]
            return acc + jnp.where(blk >= cand, one16, zero16)

        def pair_body(j, acc):
            ks = pl.multiple_of(j * (2 * qblk), 2 * qblk)
            for c in range(2 * nrep):
                acc = slab(acc, ks, c)
            return acc

        def single_body(j, acc):
            ks = pl.multiple_of(j * qblk, qblk)
            for c in range(nrep):
                acc = slab(acc, ks, c)
            return acc

        npair = nch // 2
        acc = lax.fori_loop(0, npair, pair_body, jnp.zeros((rows, LANES), BF16))
        return lax.fori_loop(2 * npair, nch, single_body, acc)

    def counts16(src_scr, cands):
        accs = [hits16(src_scr, r0, cand) for r0, cand in zip(groups, cands)]
        return [jnp.dot(acc, lane_ones, preferred_element_type=F32) for acc in accs]

    def bisect16(count_fn, targets):
        targets16 = [jnp.broadcast_to(jnp.asarray(target, F32), (rows, LANES)).astype(BF16) for target in targets]

        def body(bi, carry):
            thrs, aboves = carry
            bit = jnp.where(bi == 0, jnp.int32(I16_MIN), lax.shift_left(jnp.int32(1), 15 - bi))
            step = jnp.full((rows, LANES), bit, jnp.int32).astype(jnp.int16)
            cands = [thr ^ step for thr in thrs]
            cnts = [cnt.astype(BF16) for cnt in count_fn(cands)]
            takes = [cnt >= target for cnt, target in zip(cnts, targets16)]
            return (tuple(jnp.where(take, cand, thr) for take, cand, thr in zip(takes, cands, thrs)),
                    tuple(jnp.where(take, above, cnt) for take, above, cnt in zip(takes, aboves, cnts)))
        start = (tuple(jnp.full((rows, LANES), I16_MIN, jnp.int16) for _ in groups),
                 tuple(jnp.zeros((rows, LANES), BF16) for _ in groups))
        thrs, aboves = lax.fori_loop(0, 16, body, start)
        return list(thrs), [above.astype(F32) for above in aboves]

    min16 = jnp.full((), I16_MIN, jnp.int16)
    sub = 16 * SUBLANES
    thr_hi16, above_hi = bisect16(lambda cands: counts16(hi_scr, cands), [topk] * len(groups))
    quotas = [topk - cnt for cnt in above_hi]
    for r0, hi16 in zip(groups, thr_hi16):
        thr_scr[0, r0:r0 + rows, :] = hi16

    def collect(rr):
        thr_sub = thr_scr[0, rr:rr + sub, :]

        def body(j, carry):
            first, second, third, lost = carry
            ks = pl.multiple_of(j * qblk, qblk)
            for c in range(nrep):
                sl = (slice(rr, rr + sub), pl.ds(ks + c * LANES, LANES))
                x = jnp.where(hi_scr[sl] == thr_sub, lo_scr[sl], min16)
                gt1, gt2, gt3 = x > first, x > second, x > third
                out = jnp.where(gt3, third, x)
                third = jnp.where(gt2, second, jnp.where(gt3, x, third))
                second = jnp.where(gt1, first, jnp.where(gt2, x, second))
                first = jnp.where(gt1, x, first)
                lost = jnp.where(out > lost, out, lost)
            return first, second, third, lost

        lowest = jnp.full((sub, LANES), I16_MIN, jnp.int16)
        first, second, third, lost = lax.fori_loop(0, nch, body, (lowest,) * 4)
        top_scr[0, rr:rr + sub, :] = first
        top_scr[1, rr:rr + sub, :] = second
        top_scr[2, rr:rr + sub, :] = third
        return jnp.max(lost.astype(F32))

    lost_max = collect(0)
    for rr in range(sub, qblk, sub):
        lost_max = jnp.maximum(lost_max, collect(rr))
    lossless = lost_max == float(I16_MIN)

    def counts_top(cands):
        outs = []
        for r0, cand in zip(groups, cands):
            acc = jnp.zeros((rows, LANES), BF16)
            for t in range(3):
                acc = acc + jnp.where(top_scr[t, r0:r0 + rows, :] >= cand, one16, zero16)
            outs.append(jnp.dot(acc, lane_ones, preferred_element_type=F32))
        return outs

    def finish(thr_lo16, above_lo):
        for r0, lo16, quota, above in zip(groups, thr_lo16, quotas, above_lo):
            thr_scr[1, r0:r0 + rows, :] = lo16
            need_scr[r0:r0 + rows, :] = quota - above

    @pl.when(lossless)
    def _():
        finish(*bisect16(counts_top, quotas))

    @pl.when(jnp.logical_not(lossless))
    def _():
        def pin_body(j, carry):
            ks = pl.multiple_of(j * qblk, qblk)
            for r0, thr16 in zip(groups, thr_hi16):
                for c in range(nrep):
                    sl = (slice(r0, r0 + rows), pl.ds(ks + c * LANES, LANES))
                    lo_scr[sl] = jnp.where(hi_scr[sl] == thr16, lo_scr[sl], min16)
            return carry

        lax.fori_loop(0, nch, pin_body, 0)
        finish(*bisect16(lambda cands: counts16(lo_scr, cands), quotas))

    m_scr[...] = jnp.full(m_scr.shape, NEG_BIG, F32)
    acc_scr[...] = jnp.zeros(acc_scr.shape, F32)
    thr_hi_all = thr_scr[0]
    thr_lo_all = thr_scr[1]
    need_all = need_scr[...]
    nsub = MXU_DIM // LANES
    prefix_ones = jnp.where(lax.broadcasted_iota(jnp.int32, (MXU_DIM, MXU_DIM), 0)
                            <= lax.broadcasted_iota(jnp.int32, (MXU_DIM, MXU_DIM), 1), 1.0, 0.0).astype(BF16)
    ninf16 = jnp.full((), -jnp.inf, BF16)

    def attn_chunk(j, seen, diag):
        ks = pl.multiple_of(j * qblk, qblk)
        reach, equal = [], []
        for c in range(nrep):
            hi_c = hi_scr[:, pl.ds(ks + c * LANES, LANES)]
            lo_c = lo_scr[:, pl.ds(ks + c * LANES, LANES)]
            bucket = hi_c == thr_hi_all
            reach.append(jnp.where(hi_c > thr_hi_all, zero16,
                                   jnp.where(bucket, jnp.where(lo_c >= thr_lo_all, zero16, ninf16), ninf16)))
            equal.append(jnp.where(bucket, jnp.where(lo_c == thr_lo_all, one16, zero16), zero16))
        parts = []
        for u in range(qblk // MXU_DIM):
            eq_u = jnp.concatenate(equal[u * nsub:(u + 1) * nsub], axis=1)
            rank = jnp.dot(eq_u, prefix_ones, preferred_element_type=F32)
            room = jnp.concatenate([jnp.maximum(need_all - seen, 0.0)] * nsub, axis=1)
            over = jnp.where(rank * eq_u.astype(F32) > room, -jnp.inf, 0.0)
            parts.append(jnp.concatenate(reach[u * nsub:(u + 1) * nsub], axis=1).astype(F32) + over)
            seen = seen + jnp.max(rank, axis=-1, keepdims=True)
        bias = jnp.concatenate(parts, axis=1)
        if diag:
            bias = jnp.where(causal, bias, -jnp.inf)
        kc = k_ref[pl.ds(ks, qblk), :]
        vc = v_ref[pl.ds(ks, qblk), :]
        for h in range(ATT_HEADS):
            s = lax.dot_general(q_ref[h], kc, _NT, preferred_element_type=F32) + bias
            m_old = m_scr[h]
            m_new = jnp.maximum(m_old, jnp.max(s, axis=-1, keepdims=True))
            alpha = jnp.exp2(m_old - m_new)
            p = jnp.exp2(s - jnp.concatenate([m_new] * nrep, axis=1))
            acc_scr[h] = alpha * acc_scr[h] + jnp.dot(p.astype(BF16), vc, preferred_element_type=F32)
            m_scr[h] = m_new
        return seen

    seen = lax.fori_loop(0, qb, lambda j, seen: attn_chunk(j, seen, False), jnp.zeros((qblk, LANES), F32))
    attn_chunk(qb, seen, True)

    outs = []
    for h in range(ATT_HEADS):
        acc = acc_scr[h]
        outs.append(acc[:, :ATT_HEAD_DIM] / acc[:, ATT_HEAD_DIM:])
    o_ref[...] = jnp.concatenate(outs, axis=1).astype(o_ref.dtype)


def _dsa(q, qi, proj, ki, k, v, *, qblk=512, rows=256):
    b, _, s, _ = q.shape
    topk = min(TOPK_MAX, s // 4)
    assert qblk >= topk and s % qblk == 0 and qblk % rows == 0 and qblk % MXU_DIM == 0
    assert s // LANES <= 256 and topk <= 256
    kern = functools.partial(_dsa_kernel, qblk=qblk, topk=float(topk), rows=rows)
    head_tok = lambda bi, i: (bi, 0, i, 0)
    full = lambda bi, i: (bi, 0, 0)
    once = pl.Buffered(1)
    return pl.pallas_call(
        kern,
        grid=(b, s // qblk),
        in_specs=[
            pl.BlockSpec((None, ATT_HEADS, qblk, ATT_HEAD_DIM), head_tok),
            pl.BlockSpec((None, IDX_HEADS, qblk, IDX_DIM), head_tok),
            pl.BlockSpec((None, qblk, LANES), lambda bi, i: (bi, i, GATE_BLOCK)),
            pl.BlockSpec((None, s, IDX_DIM), full, pipeline_mode=once),
            pl.BlockSpec((None, s, ATT_HEAD_DIM), full, pipeline_mode=once),
            pl.BlockSpec((None, s, LANES), full, pipeline_mode=once),
        ],
        out_specs=pl.BlockSpec((None, qblk, ATT_WIDTH), lambda bi, i: (bi, i, 0)),
        out_shape=jax.ShapeDtypeStruct((b, s, ATT_WIDTH), BF16),
        scratch_shapes=[
            pltpu.VMEM((qblk, s + LANES), jnp.int16),
            pltpu.VMEM((qblk, s + LANES), jnp.int16),
            pltpu.VMEM((2, qblk, LANES), jnp.int16),
            pltpu.VMEM((3, qblk, LANES), jnp.int16),
            pltpu.VMEM((qblk, LANES), F32),
            pltpu.VMEM((ATT_HEADS, qblk, LANES), F32),
            pltpu.VMEM((ATT_HEADS, qblk, LANES), F32),
        ],
        compiler_params=_params("parallel", "arbitrary"),
        name="dsa_attention",
    )(q, qi, proj, ki, k, v)


def _log_sigmoid(x):
    return jnp.minimum(x, 0.0) - jnp.log(1.0 + jnp.exp(-jnp.abs(x)))


def _mlstm_kernel(qk_ref, v_ref, og_ref, g_ref, cw_ref, cb_ref, gb_ref, mn_ref, o_ref,
                  xe_scr, qk_scr, c_scr, n_scr, m_scr):
    L = CHUNK
    d = M_HEAD_DIM
    hi = lax.Precision.HIGHEST

    @pl.when(pl.program_id(1) == 0)
    def _():
        xe_scr[0:SUBLANES, :] = jnp.zeros((SUBLANES, 2 * M_WIDTH), F32)
        c_scr[...] = jnp.zeros(c_scr.shape, F32)
        n_scr[...] = jnp.zeros(n_scr.shape, F32)
        m_scr[...] = jnp.zeros(m_scr.shape, F32)

    xe_scr[SUBLANES:SUBLANES + L, :] = qk_ref[...]
    y = jnp.broadcast_to(cb_ref[...], (L, 2 * M_WIDTH))
    for j in range(CONV_K):
        off = SUBLANES - (CONV_K - 1) + j
        y = y + cw_ref[j:j + 1, :] * xe_scr[off:off + L, :]
    xe_scr[0:SUBLANES, :] = xe_scr[L:L + SUBLANES, :]
    qk_scr[...] = y * _sigmoid(y)

    g = g_ref[...] + gb_ref[...]
    ls = _log_sigmoid(g)
    row = lax.broadcasted_iota(jnp.int32, (L, L), 0)
    col = lax.broadcasted_iota(jnp.int32, (L, L), 1)
    lower = row >= col
    tril = jnp.where(lower, 1.0, 0.0)
    triu = jnp.where(row <= col, 1.0, 0.0)
    b_cols = jnp.dot(tril, ls, preferred_element_type=F32, precision=hi)
    g_t = g.T
    b_rows = jnp.dot(ls.T, triu, preferred_element_type=F32, precision=hi)

    for h in range(M_HEADS):
        sl = slice(h * d, (h + 1) * d)
        qh = qk_scr[:, sl]
        kh = qk_scr[:, M_WIDTH + h * d:M_WIDTH + (h + 1) * d] * (d ** -0.5)
        vh = v_ref[:, sl]
        qb16 = qh.astype(BF16)
        kb16 = kh.astype(BF16)
        vb16 = vh.astype(BF16)
        b_col = b_cols[:, F_LANE + h:F_LANE + h + 1]
        i_col = g[:, I_LANE + h:I_LANE + h + 1]
        b_row = b_rows[F_LANE + h:F_LANE + h + 1, :]
        i_row = g_t[I_LANE + h:I_LANE + h + 1, :]
        b_last = b_row[:, L - 1:L]

        c_prev = c_scr[h]
        n_prev = n_scr[h]
        m_prev = m_scr[h]

        a_row = b_last - b_row + i_row
        a_max = jnp.max(a_row, axis=-1, keepdims=True)
        wa_col = jnp.exp(b_last - b_col + i_col - a_max)
        kw = kh * wa_col
        c_chunk = lax.dot_general(kw.astype(BF16), vb16, _TN, preferred_element_type=F32)
        n_chunk = jnp.sum(kw, axis=0, keepdims=True)

        g_col = b_col + m_prev
        dmat = jnp.where(lower, b_col + (i_row - b_row), -jnp.inf)
        m_t = jnp.maximum(g_col, jnp.max(dmat, axis=-1, keepdims=True))
        w_intra = jnp.exp(dmat - m_t)
        w_inter = jnp.exp(g_col - m_t)
        s = lax.dot_general(qb16, kb16, _NT, preferred_element_type=F32) * w_intra
        num = w_inter * jnp.dot(qb16, c_prev.astype(BF16), preferred_element_type=F32) \
            + jnp.dot(s.astype(BF16), vb16, preferred_element_type=F32)
        den = w_inter * jnp.sum(qh * n_prev, axis=-1, keepdims=True) + jnp.sum(s, axis=-1, keepdims=True)
        hh = num / jnp.maximum(jnp.abs(den), jnp.exp(-m_t))
        hn = _rms_rows(hh, mn_ref[:, sl])
        o_ref[:, sl] = (hn * _sigmoid(og_ref[:, sl])).astype(o_ref.dtype)

        m_new = jnp.maximum(b_last + m_prev, a_max)
        s_old = jnp.exp(b_last + m_prev - m_new)
        s_new = jnp.exp(a_max - m_new)
        c_scr[h] = s_old * c_prev + s_new * c_chunk
        n_scr[h] = s_old * n_prev + s_new * n_chunk
        m_scr[h] = m_new


def _mlstm(proj, conv_w, conv_b, gate_b, m_norm):
    b, s, _ = proj.shape
    L = CHUNK
    assert s % L == 0
    cw = jnp.zeros((SUBLANES, 2 * M_WIDTH), F32).at[:CONV_K].set(conv_w.reshape(CONV_K, 2 * M_WIDTH))
    gb = jnp.zeros((1, LANES), F32).at[0, I_LANE:I_LANE + 2 * M_HEADS].set(gate_b)
    const = lambda bi, c: (0, 0)
    return pl.pallas_call(
        _mlstm_kernel,
        grid=(b, s // L),
        in_specs=[
            pl.BlockSpec((None, L, 2 * M_WIDTH), lambda bi, c: (bi, c, 1)),
            pl.BlockSpec((None, L, M_WIDTH), lambda bi, c: (bi, c, 4)),
            pl.BlockSpec((None, L, M_WIDTH), lambda bi, c: (bi, c, 5)),
            pl.BlockSpec((None, L, LANES), lambda bi, c: (bi, c, GATE_BLOCK)),
            pl.BlockSpec((SUBLANES, 2 * M_WIDTH), const),
            pl.BlockSpec((1, 2 * M_WIDTH), const),
            pl.BlockSpec((1, LANES), const),
            pl.BlockSpec((1, M_WIDTH), const),
        ],
        out_specs=pl.BlockSpec((None, L, M_WIDTH), lambda bi, c: (bi, c, 0)),
        out_shape=jax.ShapeDtypeStruct((b, s, M_WIDTH), BF16),
        scratch_shapes=[
            pltpu.VMEM((L + SUBLANES, 2 * M_WIDTH), F32),
            pltpu.VMEM((L, 2 * M_WIDTH), F32),
            pltpu.VMEM((M_HEADS, M_HEAD_DIM, M_HEAD_DIM), F32),
            pltpu.VMEM((M_HEADS, 1, M_HEAD_DIM), F32),
            pltpu.VMEM((M_HEADS, 1, 1), F32),
        ],
        compiler_params=_params("parallel", "arbitrary"),
        name="mlstm",
    )(proj, proj, proj, proj, cw, conv_b.reshape(1, -1), gb, m_norm.reshape(1, M_WIDTH))


def kernel(x, ffn1_norm, ffn1_w_gate, ffn1_w_up, ffn1_w_down, mix_norm, w_in, conv_w, conv_b, gate_b,
           q_norm, k_norm, m_norm, w_out, ffn2_norm, ffn2_w_gate, ffn2_w_up, ffn2_w_down):
    b, s, d = x.shape
    depth = w_in.shape[0]
    x2 = x.reshape(b * s, d)
    for l in range(depth):
        x2 = _ffn(x2, ffn1_norm[l], ffn1_w_gate[l], ffn1_w_up[l], ffn1_w_down[l])
        proj, q, qi, k, v, ki = _inproj(x2, mix_norm[l], _arrange_w_in(w_in[l]), q_norm[l], k_norm[l], b, s)
        proj = proj.reshape(b, s, PROJ_COLS)
        att = _dsa(q, qi, proj, ki, k, v)
        hm = _mlstm(proj, conv_w[l], conv_b[l], gate_b[l], m_norm[l])
        mix = (att.reshape(b * s, ATT_WIDTH), hm.reshape(b * s, M_WIDTH), w_out[l])
        x2 = _ffn(x2, ffn2_norm[l], ffn2_w_gate[l], ffn2_w_up[l], ffn2_w_down[l], mix)
    return x2.reshape(b, s, d)
```

```python
import functools

import jax
import jax.numpy as jnp
from jax import lax
from jax.experimental import pallas as pl
from jax.experimental.pallas import tpu as pltpu

ATT_HEADS = 8
ATT_HEAD_DIM = 64
ATT_WIDTH = ATT_HEADS * ATT_HEAD_DIM
IDX_HEADS = 4
IDX_DIM = 64
TOPK_MAX = 256
M_HEADS = 4
M_HEAD_DIM = 128
M_WIDTH = M_HEADS * M_HEAD_DIM
CHUNK = 128
CONV_K = 4
ROPE_THETA = 10000.0
EPS = 1e-6

LANES = 128
SUBLANES = 8
MXU_DIM = 256
VMEM_LIMIT_BYTES = 56 * 1024 * 1024

PROJ_COLS = 3072
DSA_COLS = 1024
GATE_BLOCK = 7
STORE_COLS = PROJ_COLS - DSA_COLS + 128
STORE_GATE_BLOCK = (PROJ_COLS - DSA_COLS) // 128
W_LANE = 64
I_LANE = 68
F_LANE = 72

I16_MIN = -(2 ** 15)
NEG_BIG = -1e30
LOG2E = 1.4426950408889634

BF16 = jnp.bfloat16
F32 = jnp.float32

_NT = (((1,), (1,)), ((), ()))
_TN = (((0,), (0,)), ((), ()))


def _params(*sem):
    return pltpu.CompilerParams(dimension_semantics=sem, vmem_limit_bytes=VMEM_LIMIT_BYTES)


def _sigmoid(x):
    return 1.0 / (1.0 + jnp.exp(-x))


def _rms_rows(x, gain):
    return x * lax.rsqrt(jnp.mean(x * x, axis=-1, keepdims=True) + EPS) * gain


def _ffn_kernel(x_ref, g_ref, wg_ref, wu_ref, wd_ref, *rest, chunks):
    o_ref = rest[-1]
    x = x_ref[...]
    if len(rest) > 1:
        att_ref, hm_ref, wo_ref = rest[:-1]
        cat = jnp.concatenate([att_ref[...], hm_ref[...]], axis=1)
        x = x + jnp.dot(cat, wo_ref[...], preferred_element_type=F32)
    h = _rms_rows(x, g_ref[...]).astype(BF16)
    acc = jnp.zeros(x.shape, F32)
    f0 = 0
    for fc in chunks:
        g = jnp.dot(h, wg_ref[:, f0:f0 + fc], preferred_element_type=F32)
        u = jnp.dot(h, wu_ref[:, f0:f0 + fc], preferred_element_type=F32)
        a = (g * _sigmoid(g) * u).astype(BF16)
        acc = acc + jnp.dot(a, wd_ref[f0:f0 + fc, :], preferred_element_type=F32)
        f0 += fc
    o_ref[...] = x + 0.5 * acc


def _ffn_chunks(d_ff):
    step = 512
    chunks = [step] * (d_ff // step)
    if d_ff % step:
        chunks.append(d_ff % step)
    return tuple(chunks)


def _ffn(x2, gain, w_gate, w_up, w_down, mix=None, *, tm=512):
    t, d = x2.shape
    d_ff = w_gate.shape[1]
    const = lambda i: (0, 0)
    rows = lambda i: (i, 0)
    once = pl.Buffered(1)
    operands = [x2, gain.reshape(1, d), w_gate.astype(BF16), w_up.astype(BF16), w_down.astype(BF16)]
    in_specs = [
        pl.BlockSpec((tm, d), rows),
        pl.BlockSpec((1, d), const),
        pl.BlockSpec((d, d_ff), const, pipeline_mode=once),
        pl.BlockSpec((d, d_ff), const, pipeline_mode=once),
        pl.BlockSpec((d_ff, d), const, pipeline_mode=once),
    ]
    if mix is not None:
        att2, hm2, w_out = mix
        operands += [att2, hm2, w_out.astype(BF16)]
        in_specs += [
            pl.BlockSpec((tm, att2.shape[1]), rows),
            pl.BlockSpec((tm, hm2.shape[1]), rows),
            pl.BlockSpec((att2.shape[1] + hm2.shape[1], d), const, pipeline_mode=once),
        ]
    return pl.pallas_call(
        functools.partial(_ffn_kernel, chunks=_ffn_chunks(d_ff)),
        grid=(t // tm,),
        in_specs=in_specs,
        out_specs=pl.BlockSpec((tm, d), rows),
        out_shape=jax.ShapeDtypeStruct((t, d), F32),
        compiler_params=_params("parallel"),
        name="ffn" if mix is None else "mix_ffn",
    )(*operands)


def _inproj_kernel(x_ref, g_ref, w_ref, cos_ref, sin_ref, gmat_ref, qn_ref, kn_ref,
                   o_ref, q_ref, qi_ref, k_ref, v_ref, ki_ref):
    h = _rms_rows(x_ref[...], g_ref[...]).astype(BF16)
    head = DSA_COLS
    p = jnp.dot(h, w_ref[:, :head], preferred_element_type=F32)
    o_ref[:, :PROJ_COLS - head] = jnp.dot(h, w_ref[:, head:], preferred_element_type=F32)
    o_ref[:, PROJ_COLS - head:] = p[:, GATE_BLOCK * LANES:(GATE_BLOCK + 1) * LANES]
    _dsa_operands(p, cos_ref[...], sin_ref[...], gmat_ref[...], qn_ref[...], kn_ref[...],
                  q_ref, qi_ref, k_ref, v_ref, ki_ref)


def _inproj(x2, gain, w_arranged, q_norm, k_norm, b, s, *, tm=512):
    t, d = x2.shape
    assert s % tm == 0
    nblk = s // tm
    cosf, sinf = _rope_tables(s)
    lane = jnp.arange(LANES)
    gmat = (lane[:, None] // ATT_HEAD_DIM == lane[None, :] // ATT_HEAD_DIM).astype(BF16)
    qn = jnp.concatenate([q_norm, q_norm]).reshape(1, LANES)
    kn = jnp.concatenate([k_norm, jnp.ones_like(k_norm)]).reshape(1, LANES)
    const = lambda i: (0, 0)
    pos = lambda i: (i % nblk, 0)
    tok = lambda i: (i // nblk, i % nblk, 0)
    head_tok = lambda i: (i // nblk, 0, i % nblk, 0)
    return pl.pallas_call(
        _inproj_kernel,
        grid=(t // tm,),
        in_specs=[
            pl.BlockSpec((tm, d), lambda i: (i, 0)),
            pl.BlockSpec((1, d), const),
            pl.BlockSpec((d, PROJ_COLS), const, pipeline_mode=pl.Buffered(1)),
            pl.BlockSpec((tm, LANES), pos),
            pl.BlockSpec((tm, LANES), pos),
            pl.BlockSpec((LANES, LANES), const),
            pl.BlockSpec((1, LANES), const),
            pl.BlockSpec((1, LANES), const),
        ],
        out_specs=[
            pl.BlockSpec((tm, STORE_COLS), lambda i: (i, 0)),
            pl.BlockSpec((None, ATT_HEADS, tm, ATT_HEAD_DIM), head_tok),
            pl.BlockSpec((None, IDX_HEADS, tm, IDX_DIM), head_tok),
            pl.BlockSpec((None, tm, ATT_HEAD_DIM), tok),
            pl.BlockSpec((None, tm, LANES), tok),
            pl.BlockSpec((None, tm, IDX_DIM), tok),
        ],
        out_shape=[
            jax.ShapeDtypeStruct((t, STORE_COLS), F32),
            jax.ShapeDtypeStruct((b, ATT_HEADS, s, ATT_HEAD_DIM), BF16),
            jax.ShapeDtypeStruct((b, IDX_HEADS, s, IDX_DIM), BF16),
            jax.ShapeDtypeStruct((b, s, ATT_HEAD_DIM), BF16),
            jax.ShapeDtypeStruct((b, s, LANES), BF16),
            jax.ShapeDtypeStruct((b, s, IDX_DIM), BF16),
        ],
        compiler_params=_params("parallel"),
        name="inproj",
    )(x2, gain.reshape(1, d), w_arranged, cosf, sinf, gmat, qn, kn)


def _arrange_w_in(w_in):
    d = w_in.shape[0]
    head = ATT_WIDTH + 2 * ATT_HEAD_DIM + IDX_HEADS * IDX_DIM + IDX_DIM + IDX_HEADS
    body = 4 * M_WIDTH
    gates = w_in[:, head + body:head + body + 2 * M_HEADS]
    pad = jnp.zeros((d, PROJ_COLS - head - body - 2 * M_HEADS), w_in.dtype)
    return jnp.concatenate([w_in[:, :head], gates, pad, w_in[:, head:head + body]], axis=1).astype(BF16)


def _rope(x, cosf, sinf):
    lane = lax.broadcasted_iota(jnp.int32, x.shape, 1)
    lower = (lane % ATT_HEAD_DIM) < (ATT_HEAD_DIM // 2)
    width = x.shape[1]
    partner = jnp.where(lower, pltpu.roll(x, width - ATT_HEAD_DIM // 2, 1), pltpu.roll(x, ATT_HEAD_DIM // 2, 1))
    return x * cosf + partner * sinf


def _group_mean_sq(x, gmat):
    sq = x * x
    hi = sq.astype(BF16)
    lo = (sq - hi.astype(F32)).astype(BF16)
    tot = jnp.dot(hi, gmat, preferred_element_type=F32) + jnp.dot(lo, gmat, preferred_element_type=F32)
    return tot * (1.0 / ATT_HEAD_DIM)


def _dsa_operands(p, cosf, sinf, gmat, qn, kn, q_ref, qi_ref, k_ref, v_ref, ki_ref):
    qk_scale = ATT_HEAD_DIM ** -0.5 * LOG2E
    idx_scale = IDX_DIM ** -0.5
    for s in range(ATT_WIDTH // LANES):
        x = p[:, s * LANES:(s + 1) * LANES]
        xn = x * lax.rsqrt(_group_mean_sq(x, gmat) + EPS) * qn
        xr = (_rope(xn, cosf, sinf) * qk_scale).astype(BF16)
        q_ref[2 * s] = xr[:, :ATT_HEAD_DIM]
        q_ref[2 * s + 1] = xr[:, ATT_HEAD_DIM:]
    kv = p[:, ATT_WIDTH:ATT_WIDTH + LANES]
    kn = kv * lax.rsqrt(_group_mean_sq(kv, gmat) + EPS) * kn
    kr = _rope(kn, cosf, sinf)
    k_ref[...] = kr[:, :ATT_HEAD_DIM].astype(BF16)
    lane = lax.broadcasted_iota(jnp.int32, kv.shape, 1)
    v_ref[...] = jnp.where(lane < ATT_HEAD_DIM, pltpu.roll(kv, ATT_HEAD_DIM, 1), 1.0).astype(BF16)
    base = ATT_WIDTH + LANES
    for s in range(IDX_HEADS * IDX_DIM // LANES):
        x = p[:, base + s * LANES:base + (s + 1) * LANES]
        xr = (_rope(x, cosf, sinf) * idx_scale).astype(BF16)
        qi_ref[2 * s] = xr[:, :IDX_DIM]
        qi_ref[2 * s + 1] = xr[:, IDX_DIM:]
    gk = p[:, GATE_BLOCK * LANES:(GATE_BLOCK + 1) * LANES]
    ki_ref[...] = _rope(gk, cosf, sinf)[:, :IDX_DIM].astype(BF16)


def _rope_tables(seq):
    half = ATT_HEAD_DIM // 2
    inv_freq = ROPE_THETA ** (-jnp.arange(half, dtype=F32) / half)
    ang = jnp.arange(seq, dtype=jnp.int32).astype(F32)[:, None] * inv_freq[None, :]
    cos = jnp.cos(ang)
    sin = jnp.sin(ang)
    cosf = jnp.concatenate([cos, cos, cos, cos], axis=1)
    sinf = jnp.concatenate([-sin, sin, -sin, sin], axis=1)
    return cosf, sinf


def _sortable_key(score):
    bits = lax.bitcast_convert_type(score, jnp.int32)
    sign = bits >> 31
    return (bits ^ (sign & jnp.int32(0x7FFFFFFF))) - sign


def _dsa_kernel(q_ref, qi_ref, w_ref, ki_ref, k_ref, v_ref, o_ref,
                hi_scr, lo_scr, thr_scr, top_scr, need_scr, m_scr, acc_scr, *, qblk, topk, rows):
    qb = pl.program_id(1)
    nch = qb + 1
    nrep = qblk // LANES
    row = lax.broadcasted_iota(jnp.int32, (qblk, qblk), 0)
    col = lax.broadcasted_iota(jnp.int32, (qblk, qblk), 1)
    causal = col <= row

    wblk = w_ref[...]
    wcols = [jnp.broadcast_to(wblk[:, W_LANE + h:W_LANE + h + 1] * (IDX_HEADS ** -0.5), (qblk, qblk))
             for h in range(IDX_HEADS)]

    def score_chunk(j, diag):
        ks = pl.multiple_of(j * qblk, qblk)
        kc = ki_ref[pl.ds(ks, qblk), :]
        sc = jnp.zeros((qblk, qblk), F32)
        for h in range(IDX_HEADS):
            logit = lax.dot_general(qi_ref[h], kc, _NT, preferred_element_type=F32)
            sc = sc + jnp.maximum(logit, 0.0) * wcols[h]
        if diag:
            sc = jnp.where(causal, sc, -jnp.inf)
        key = _sortable_key(sc)
        hi_scr[:, pl.ds(ks, qblk)] = (key >> 16).astype(jnp.int16)
        lo_scr[:, pl.ds(ks, qblk)] = ((key & 0xFFFF) + I16_MIN).astype(jnp.int16)

    def score_body(j, carry):
        score_chunk(j, False)
        return carry

    lax.fori_loop(0, qb, score_body, 0)
    score_chunk(qb, True)

    one16 = jnp.ones((), BF16)
    zero16 = jnp.zeros((), BF16)
    lane_ones = jnp.ones((LANES, LANES), BF16)

    groups = list(range(0, qblk, rows))

    def hits16(src_scr, r0, cand):
        def slab(acc, ks, c):
            blk = src_scr[r0:r0 + rows, pl.ds(ks + c * LANES, LANES)]
            return acc + jnp.where(blk >= cand, one16, zero16)

        def pair_body(j, acc):
            ks = pl.multiple_of(j * (2 * qblk), 2 * qblk)
            for c in range(2 * nrep):
                acc = slab(acc, ks, c)
            return acc

        def single_body(j, acc):
            ks = pl.multiple_of(j * qblk, qblk)
            for c in range(nrep):
                acc = slab(acc, ks, c)
            return acc

        npair = nch // 2
        acc = lax.fori_loop(0, npair, pair_body, jnp.zeros((rows, LANES), BF16))
        return lax.fori_loop(2 * npair, nch, single_body, acc)

    def counts16(src_scr, cands):
        accs = [hits16(src_scr, r0, cand) for r0, cand in zip(groups, cands)]
        return [jnp.dot(acc, lane_ones, preferred_element_type=F32) for acc in accs]

    def bisect16(count_fn, targets):
        targets16 = [jnp.broadcast_to(jnp.asarray(target, F32), (rows, LANES)).astype(BF16) for target in targets]

        def body(bi, carry):
            thrs, aboves = carry
            bit = jnp.where(bi == 0, jnp.int32(I16_MIN), lax.shift_left(jnp.int32(1), 15 - bi))
            step = jnp.full((rows, LANES), bit, jnp.int32).astype(jnp.int16)
            cands = [thr ^ step for thr in thrs]
            cnts = [cnt.astype(BF16) for cnt in count_fn(cands)]
            takes = [cnt >= target for cnt, target in zip(cnts, targets16)]
            return (tuple(jnp.where(take, cand, thr) for take, cand, thr in zip(takes, cands, thrs)),
                    tuple(jnp.where(take, above, cnt) for take, above, cnt in zip(takes, aboves, cnts)))
        start = (tuple(jnp.full((rows, LANES), I16_MIN, jnp.int16) for _ in groups),
                 tuple(jnp.zeros((rows, LANES), BF16) for _ in groups))
        thrs, aboves = lax.fori_loop(0, 16, body, start)
        return list(thrs), [above.astype(F32) for above in aboves]

    min16 = jnp.full((), I16_MIN, jnp.int16)
    sub = 16 * SUBLANES
    thr_hi16, above_hi = bisect16(lambda cands: counts16(hi_scr, cands), [topk] * len(groups))
    quotas = [topk - cnt for cnt in above_hi]
    for r0, hi16 in zip(groups, thr_hi16):
        thr_scr[0, r0:r0 + rows, :] = hi16

    def collect(rr):
        thr_sub = thr_scr[0, rr:rr + sub, :]

        def body(j, carry):
            first, second, third, lost = carry
            ks = pl.multiple_of(j * qblk, qblk)
            for c in range(nrep):
                sl = (slice(rr, rr + sub), pl.ds(ks + c * LANES, LANES))
                x = jnp.where(hi_scr[sl] == thr_sub, lo_scr[sl], min16)
                gt1, gt2, gt3 = x > first, x > second, x > third
                out = jnp.where(gt3, third, x)
                third = jnp.where(gt2, second, jnp.where(gt3, x, third))
                second = jnp.where(gt1, first, jnp.where(gt2, x, second))
                first = jnp.where(gt1, x, first)
                lost = jnp.where(out > lost, out, lost)
            return first, second, third, lost

        lowest = jnp.full((sub, LANES), I16_MIN, jnp.int16)
        first, second, third, lost = lax.fori_loop(0, nch, body, (lowest,) * 4)
        top_scr[0, rr:rr + sub, :] = first
        top_scr[1, rr:rr + sub, :] = second
        top_scr[2, rr:rr + sub, :] = third
        return jnp.max(lost.astype(F32))

    lost_max = collect(0)
    for rr in range(sub, qblk, sub):
        lost_max = jnp.maximum(lost_max, collect(rr))
    lossless = lost_max == float(I16_MIN)

    def counts_top(cands):
        outs = []
        for r0, cand in zip(groups, cands):
            acc = jnp.zeros((rows, LANES), BF16)
            for t in range(3):
                acc = acc + jnp.where(top_scr[t, r0:r0 + rows, :] >= cand, one16, zero16)
            outs.append(jnp.dot(acc, lane_ones, preferred_element_type=F32))
        return outs

    def finish(thr_lo16, above_lo):
        for r0, lo16, quota, above in zip(groups, thr_lo16, quotas, above_lo):
            thr_scr[1, r0:r0 + rows, :] = lo16
            need_scr[r0:r0 + rows, :] = quota - above

    @pl.when(lossless)
    def _():
        finish(*bisect16(counts_top, quotas))

    @pl.when(jnp.logical_not(lossless))
    def _():
        def pin_body(j, carry):
            ks = pl.multiple_of(j * qblk, qblk)
            for r0, thr16 in zip(groups, thr_hi16):
                for c in range(nrep):
                    sl = (slice(r0, r0 + rows), pl.ds(ks + c * LANES, LANES))
                    lo_scr[sl] = jnp.where(hi_scr[sl] == thr16, lo_scr[sl], min16)
            return carry

        lax.fori_loop(0, nch, pin_body, 0)
        finish(*bisect16(lambda cands: counts16(lo_scr, cands), quotas))

    m_scr[...] = jnp.full(m_scr.shape, NEG_BIG, F32)
    acc_scr[...] = jnp.zeros(acc_scr.shape, F32)
    nsub = MXU_DIM // LANES
    prefix_ones = jnp.where(lax.broadcasted_iota(jnp.int32, (MXU_DIM, MXU_DIM), 0)
                            <= lax.broadcasted_iota(jnp.int32, (MXU_DIM, MXU_DIM), 1), 1.0, 0.0).astype(BF16)
    ninf16 = jnp.full((), -jnp.inf, BF16)

    def attn_chunk(j, seen, diag):
        ks = pl.multiple_of(j * qblk, qblk)
        thr_hi_all = thr_scr[0]
        thr_lo_all = thr_scr[1]
        need_all = need_scr[...]
        reach, equal = [], []
        for c in range(nrep):
            hi_c = hi_scr[:, pl.ds(ks + c * LANES, LANES)]
            lo_c = lo_scr[:, pl.ds(ks + c * LANES, LANES)]
            bucket = hi_c == thr_hi_all
            reach.append(jnp.where(hi_c > thr_hi_all, zero16,
                                   jnp.where(bucket, jnp.where(lo_c >= thr_lo_all, zero16, ninf16), ninf16)))
            equal.append(jnp.where(bucket, jnp.where(lo_c == thr_lo_all, one16, zero16), zero16))
        parts = []
        for u in range(qblk // MXU_DIM):
            eq_u = jnp.concatenate(equal[u * nsub:(u + 1) * nsub], axis=1)
            rank = jnp.dot(eq_u, prefix_ones, preferred_element_type=F32)
            room = jnp.concatenate([jnp.maximum(need_all - seen, 0.0)] * nsub, axis=1)
            over = jnp.where(rank * eq_u.astype(F32) > room, -jnp.inf, 0.0)
            parts.append(jnp.concatenate(reach[u * nsub:(u + 1) * nsub], axis=1).astype(F32) + over)
            seen = seen + jnp.max(rank, axis=-1, keepdims=True)
        bias = jnp.concatenate(parts, axis=1)
        if diag:
            bias = jnp.where(causal, bias, -jnp.inf)
        kc = k_ref[pl.ds(ks, qblk), :]
        vc = v_ref[pl.ds(ks, qblk), :]
        for h in range(ATT_HEADS):
            s = lax.dot_general(q_ref[h], kc, _NT, preferred_element_type=F32) + bias
            m_old = m_scr[h]
            m_new = jnp.maximum(m_old, jnp.max(s, axis=-1, keepdims=True))
            alpha = jnp.exp2(m_old - m_new)
            p = jnp.exp2(s - jnp.concatenate([m_new] * nrep, axis=1))
            acc_scr[h] = alpha * acc_scr[h] + jnp.dot(p.astype(BF16), vc, preferred_element_type=F32)
            m_scr[h] = m_new
        return seen

    seen = lax.fori_loop(0, qb, lambda j, seen: attn_chunk(j, seen, False), jnp.zeros((qblk, LANES), F32))
    attn_chunk(qb, seen, True)

    outs = []
    for h in range(ATT_HEADS):
        acc = acc_scr[h]
        outs.append(acc[:, :ATT_HEAD_DIM] / acc[:, ATT_HEAD_DIM:])
    o_ref[...] = jnp.concatenate(outs, axis=1).astype(o_ref.dtype)


def _dsa(q, qi, proj, ki, k, v, *, qblk=512, rows=256):
    b, _, s, _ = q.shape
    topk = min(TOPK_MAX, s // 4)
    assert qblk >= topk and s % qblk == 0 and qblk % rows == 0 and qblk % MXU_DIM == 0
    assert s // LANES <= 256 and topk <= 256
    kern = functools.partial(_dsa_kernel, qblk=qblk, topk=float(topk), rows=rows)
    head_tok = lambda bi, i: (bi, 0, i, 0)
    full = lambda bi, i: (bi, 0, 0)
    once = pl.Buffered(1)
    return pl.pallas_call(
        kern,
        grid=(b, s // qblk),
        in_specs=[
            pl.BlockSpec((None, ATT_HEADS, qblk, ATT_HEAD_DIM), head_tok),
            pl.BlockSpec((None, IDX_HEADS, qblk, IDX_DIM), head_tok),
            pl.BlockSpec((None, qblk, LANES), lambda bi, i: (bi, i, STORE_GATE_BLOCK)),
            pl.BlockSpec((None, s, IDX_DIM), full, pipeline_mode=once),
            pl.BlockSpec((None, s, ATT_HEAD_DIM), full, pipeline_mode=once),
            pl.BlockSpec((None, s, LANES), full, pipeline_mode=once),
        ],
        out_specs=pl.BlockSpec((None, qblk, ATT_WIDTH), lambda bi, i: (bi, i, 0)),
        out_shape=jax.ShapeDtypeStruct((b, s, ATT_WIDTH), BF16),
        scratch_shapes=[
            pltpu.VMEM((qblk, s + LANES), jnp.int16),
            pltpu.VMEM((qblk, s + LANES), jnp.int16),
            pltpu.VMEM((2, qblk, LANES), jnp.int16),
            pltpu.VMEM((3, qblk, LANES), jnp.int16),
            pltpu.VMEM((qblk, LANES), F32),
            pltpu.VMEM((ATT_HEADS, qblk, LANES), F32),
            pltpu.VMEM((ATT_HEADS, qblk, LANES), F32),
        ],
        compiler_params=_params("parallel", "arbitrary"),
        name="dsa_attention",
    )(q, qi, proj, ki, k, v)


def _log_sigmoid(x):
    return jnp.minimum(x, 0.0) - jnp.log(1.0 + jnp.exp(-jnp.abs(x)))


def _mlstm_kernel(qk_ref, v_ref, og_ref, g_ref, cw_ref, cb_ref, gb_ref, mn_ref, o_ref,
                  xe_scr, qk_scr, c_scr, n_scr, m_scr):
    L = CHUNK
    d = M_HEAD_DIM
    hi = lax.Precision.HIGHEST

    @pl.when(pl.program_id(1) == 0)
    def _():
        xe_scr[0:SUBLANES, :] = jnp.zeros((SUBLANES, 2 * M_WIDTH), F32)
        c_scr[...] = jnp.zeros(c_scr.shape, F32)
        n_scr[...] = jnp.zeros(n_scr.shape, F32)
        m_scr[...] = jnp.zeros(m_scr.shape, F32)

    xe_scr[SUBLANES:SUBLANES + L, :] = qk_ref[...]
    y = jnp.broadcast_to(cb_ref[...], (L, 2 * M_WIDTH))
    for j in range(CONV_K):
        off = SUBLANES - (CONV_K - 1) + j
        y = y + cw_ref[j:j + 1, :] * xe_scr[off:off + L, :]
    xe_scr[0:SUBLANES, :] = xe_scr[L:L + SUBLANES, :]
    qk_scr[...] = y * _sigmoid(y)

    g = g_ref[...] + gb_ref[...]
    ls = _log_sigmoid(g)
    row = lax.broadcasted_iota(jnp.int32, (L, L), 0)
    col = lax.broadcasted_iota(jnp.int32, (L, L), 1)
    lower = row >= col
    tril = jnp.where(lower, 1.0, 0.0)
    triu = jnp.where(row <= col, 1.0, 0.0)
    b_cols = jnp.dot(tril, ls, preferred_element_type=F32, precision=hi)
    g_t = g.T
    b_rows = jnp.dot(ls.T, triu, preferred_element_type=F32, precision=hi)

    for h in range(M_HEADS):
        sl = slice(h * d, (h + 1) * d)
        qh = qk_scr[:, sl]
        kh = qk_scr[:, M_WIDTH + h * d:M_WIDTH + (h + 1) * d] * (d ** -0.5)
        vh = v_ref[:, sl]
        qb16 = qh.astype(BF16)
        kb16 = kh.astype(BF16)
        vb16 = vh.astype(BF16)
        b_col = b_cols[:, F_LANE + h:F_LANE + h + 1]
        i_col = g[:, I_LANE + h:I_LANE + h + 1]
        b_row = b_rows[F_LANE + h:F_LANE + h + 1, :]
        i_row = g_t[I_LANE + h:I_LANE + h + 1, :]
        b_last = b_row[:, L - 1:L]

        c_prev = c_scr[h]
        n_prev = n_scr[h]
        m_prev = m_scr[h]

        a_row = b_last - b_row + i_row
        a_max = jnp.max(a_row, axis=-1, keepdims=True)
        wa_col = jnp.exp(b_last - b_col + i_col - a_max)
        kw = kh * wa_col
        c_chunk = lax.dot_general(kw.astype(BF16), vb16, _TN, preferred_element_type=F32)
        n_chunk = jnp.sum(kw, axis=0, keepdims=True)

        g_col = b_col + m_prev
        dmat = jnp.where(lower, b_col + (i_row - b_row), -jnp.inf)
        m_t = jnp.maximum(g_col, jnp.max(dmat, axis=-1, keepdims=True))
        w_intra = jnp.exp(dmat - m_t)
        w_inter = jnp.exp(g_col - m_t)
        s = lax.dot_general(qb16, kb16, _NT, preferred_element_type=F32) * w_intra
        num = w_inter * jnp.dot(qb16, c_prev.astype(BF16), preferred_element_type=F32) \
            + jnp.dot(s.astype(BF16), vb16, preferred_element_type=F32)
        den = w_inter * jnp.sum(qh * n_prev, axis=-1, keepdims=True) + jnp.sum(s, axis=-1, keepdims=True)
        hh = num / jnp.maximum(jnp.abs(den), jnp.exp(-m_t))
        hn = _rms_rows(hh, mn_ref[:, sl])
        o_ref[:, sl] = (hn * _sigmoid(og_ref[:, sl])).astype(o_ref.dtype)

        m_new = jnp.maximum(b_last + m_prev, a_max)
        s_old = jnp.exp(b_last + m_prev - m_new)
        s_new = jnp.exp(a_max - m_new)
        c_scr[h] = s_old * c_prev + s_new * c_chunk
        n_scr[h] = s_old * n_prev + s_new * n_chunk
        m_scr[h] = m_new


def _mlstm(proj, conv_w, conv_b, gate_b, m_norm):
    b, s, _ = proj.shape
    L = CHUNK
    assert s % L == 0
    cw = jnp.zeros((SUBLANES, 2 * M_WIDTH), F32).at[:CONV_K].set(conv_w.reshape(CONV_K, 2 * M_WIDTH))
    gb = jnp.zeros((1, LANES), F32).at[0, I_LANE:I_LANE + 2 * M_HEADS].set(gate_b)
    const = lambda bi, c: (0, 0)
    return pl.pallas_call(
        _mlstm_kernel,
        grid=(b, s // L),
        in_specs=[
            pl.BlockSpec((None, L, 2 * M_WIDTH), lambda bi, c: (bi, c, 0)),
            pl.BlockSpec((None, L, M_WIDTH), lambda bi, c: (bi, c, 2)),
            pl.BlockSpec((None, L, M_WIDTH), lambda bi, c: (bi, c, 3)),
            pl.BlockSpec((None, L, LANES), lambda bi, c: (bi, c, STORE_GATE_BLOCK)),
            pl.BlockSpec((SUBLANES, 2 * M_WIDTH), const),
            pl.BlockSpec((1, 2 * M_WIDTH), const),
            pl.BlockSpec((1, LANES), const),
            pl.BlockSpec((1, M_WIDTH), const),
        ],
        out_specs=pl.BlockSpec((None, L, M_WIDTH), lambda bi, c: (bi, c, 0)),
        out_shape=jax.ShapeDtypeStruct((b, s, M_WIDTH), BF16),
        scratch_shapes=[
            pltpu.VMEM((L + SUBLANES, 2 * M_WIDTH), F32),
            pltpu.VMEM((L, 2 * M_WIDTH), F32),
            pltpu.VMEM((M_HEADS, M_HEAD_DIM, M_HEAD_DIM), F32),
            pltpu.VMEM((M_HEADS, 1, M_HEAD_DIM), F32),
            pltpu.VMEM((M_HEADS, 1, 1), F32),
        ],
        compiler_params=_params("parallel", "arbitrary"),
        name="mlstm",
    )(proj, proj, proj, proj, cw, conv_b.reshape(1, -1), gb, m_norm.reshape(1, M_WIDTH))


def kernel(x, ffn1_norm, ffn1_w_gate, ffn1_w_up, ffn1_w_down, mix_norm, w_in, conv_w, conv_b, gate_b,
           q_norm, k_norm, m_norm, w_out, ffn2_norm, ffn2_w_gate, ffn2_w_up, ffn2_w_down):
    b, s, d = x.shape
    depth = w_in.shape[0]
    x2 = x.reshape(b * s, d)
    for l in range(depth):
        x2 = _ffn(x2, ffn1_norm[l], ffn1_w_gate[l], ffn1_w_up[l], ffn1_w_down[l])
        proj, q, qi, k, v, ki = _inproj(x2, mix_norm[l], _arrange_w_in(w_in[l]), q_norm[l], k_norm[l], b, s)
        proj = proj.reshape(b, s, STORE_COLS)
        att = _dsa(q, qi, proj, ki, k, v)
        hm = _mlstm(proj, conv_w[l], conv_b[l], gate_b[l], m_norm[l])
        mix = (att.reshape(b * s, ATT_WIDTH), hm.reshape(b * s, M_WIDTH), w_out[l])
        x2 = _ffn(x2, ffn2_norm[l], ffn2_w_gate[l], ffn2_w_up[l], ffn2_w_down[l], mix)
    return x2.reshape(b, s, d)
```
